```python
import math
import jax, jax.numpy as jnp
from jax import lax
import numpy as np

D_MODEL = 1024
BATCH = 2
SEQ = 16384
DEPTH = 2

CTX_LEN = 256
GRID_W = 64
DA_HEADS = 4
DA_HEAD_DIM = 64
DA_V_DIM = 2 * DA_HEAD_DIM
DA_WIDTH = DA_HEADS * DA_V_DIM
SSD_HEADS = 8
SSD_HEAD_DIM = 64
SSD_WIDTH = SSD_HEADS * SSD_HEAD_DIM
SSD_GROUPS = 2
SSD_STATE = 128
SSD_CONV = 5
SSD_CHUNK = 128
SSD_CONV_CH = SSD_WIDTH + 2 * SSD_GROUPS * SSD_STATE
D_MIX = DA_WIDTH + SSD_WIDTH
D_FF = ((8 * D_MODEL + 3 * 256 - 1) // (3 * 256)) * 256
Q_BLOCK = 128
ROPE_BASE = 10000.0
EPS = 1e-6
N_MOD = 6
QK_COLS = DA_HEADS * 2 * DA_HEAD_DIM
Q_END = QK_COLS
K_END = Q_END + QK_COLS
V_END = K_END + DA_WIDTH
Z_END = V_END + SSD_WIDTH
XBC_END = Z_END + SSD_CONV_CH
IN_COLS = XBC_END + 2 * SSD_HEADS

kernel_name = "hymba_diffattn_ssd_prefix_dit"


def rmsnorm(u, g):
    uf = u.astype(jnp.float32)
    y = uf * lax.rsqrt(jnp.mean(uf * uf, axis=-1, keepdims=True) + EPS)
    return (y * g.astype(jnp.float32)).astype(u.dtype)


def modulate(u, shift, scale):
    return u * (1 + scale) + shift


def axial_rope_tables(L):
    rows = L // GRID_W
    row = jnp.repeat(jnp.arange(rows, dtype=jnp.float32), GRID_W)
    col = jnp.tile(jnp.arange(GRID_W, dtype=jnp.float32), rows)
    nf = DA_HEAD_DIM // 4
    inv = ROPE_BASE ** (-jnp.arange(nf, dtype=jnp.float32) / nf)
    ang = jnp.concatenate([row[:, None] * inv, col[:, None] * inv], axis=-1)
    return jnp.cos(ang), jnp.sin(ang)


def apply_rope(u, cos, sin):
    u1, u2 = jnp.split(u, 2, axis=-1)
    c = cos[None, :, None, None, :].astype(u.dtype)
    s = sin[None, :, None, None, :].astype(u.dtype)
    return jnp.concatenate([u1 * c - u2 * s, u1 * s + u2 * c], axis=-1)


def dwconv_centred(u, w, b):
    ch = u.shape[-1]
    out = lax.conv_general_dilated(
        u, w[:, None, :].astype(u.dtype), window_strides=(1,),
        padding=[(SSD_CONV // 2, SSD_CONV // 2)],
        dimension_numbers=('NWC', 'WIO', 'NWC'), feature_group_count=ch)
    return jax.nn.silu(out + b)


def diff_attn_block(q, k, v, lam):
    s = jnp.einsum('bqhmd,bkhmd->bhmqk', q, k).astype(jnp.float32) * (DA_HEAD_DIM ** -0.5)
    p = jax.nn.softmax(s, axis=-1)
    pd = p[:, :, 0] - lam * p[:, :, 1]
    return jnp.einsum('bhqk,bkhe->bqhe', pd.astype(v.dtype), v)


def diff_attn_latent(q, k_all, v_all, lam):
    bsz, L, H, _, d = q.shape
    nb = L // Q_BLOCK
    qb = jnp.moveaxis(q.reshape(bsz, nb, Q_BLOCK, H, 2, d), 1, 0)
    o = lax.map(lambda qi: diff_attn_block(qi, k_all, v_all, lam), qb)
    return jnp.moveaxis(o, 0, 1).reshape(bsz, L, H, DA_V_DIM)


def ssd_scan(x, dt, A, Bm, Cm, h0):
    bsz, L, H, P = x.shape
    G, N = Bm.shape[2], Bm.shape[3]
    R = H // G
    nc = L // SSD_CHUNK

    def chunks(a):
        return jnp.moveaxis(a.reshape(bsz, nc, SSD_CHUNK, *a.shape[2:]), 1, 0)

    xs = chunks(x.astype(jnp.float32).reshape(bsz, L, G, R, P))
    dts = chunks(dt.reshape(bsz, L, G, R))
    bs = chunks(Bm.astype(jnp.float32))
    cs = chunks(Cm.astype(jnp.float32))
    a_g = A.reshape(G, R)
    lower = jnp.tril(jnp.ones((SSD_CHUNK, SSD_CHUNK), dtype=bool))[None, :, :, None, None]

    def step(h, inp):
        xc, dtc, bc, cc = inp
        cum = jnp.cumsum(dtc * a_g, axis=1)
        seg = cum[:, :, None] - cum[:, None, :]
        lmat = jnp.exp(jnp.where(lower, seg, -jnp.inf))
        xdt = xc * dtc[..., None]
        cb = jnp.einsum('bign,bjgn->bijg', cc, bc)
        y_diag = jnp.einsum('bijg,bijgr,bjgrp->bigrp', cb, lmat, xdt)
        y_off = jnp.einsum('bign,bgrpn,bigr->bigrp', cc, h, jnp.exp(cum))
        decay_end = jnp.exp(cum[:, -1:] - cum)
        h_new = h * jnp.exp(cum[:, -1])[..., None, None] + jnp.einsum(
            'bjgn,bjgr,bjgrp->bgrpn', bc, decay_end, xdt)
        return h_new, y_diag + y_off

    h_fin, ys = lax.scan(step, h0, (xs, dts, bs, cs))
    y = jnp.moveaxis(ys, 0, 1).reshape(bsz, L, H, P)
    return y, h_fin


def ssd_inputs(xbc, dtr, dt_bias):
    bsz, L, _ = xbc.shape
    gn = SSD_GROUPS * SSD_STATE
    xs = xbc[..., :SSD_WIDTH].reshape(bsz, L, SSD_HEADS, SSD_HEAD_DIM)
    bm = xbc[..., SSD_WIDTH:SSD_WIDTH + gn].reshape(bsz, L, SSD_GROUPS, SSD_STATE)
    cm = xbc[..., SSD_WIDTH + gn:].reshape(bsz, L, SSD_GROUPS, SSD_STATE)
    dt = jax.nn.softplus(dtr.astype(jnp.float32).reshape(bsz, L, 2, SSD_HEADS)
                         + dt_bias.astype(jnp.float32))
    return xs, bm, cm, dt


def gated_out_norm(y, z, g):
    bsz, L = z.shape[0], z.shape[1]
    u = y.reshape(bsz, L, SSD_WIDTH) * jax.nn.silu(z.astype(jnp.float32))
    return rmsnorm(u, g).astype(z.dtype)


def hybrid_mixer(h_l, h_c, w_in, conv_w, conv_b, a_log, dt_bias, d_skip, ssd_g,
                 lam_vec, sub_g, w_out, lam_init, cos, sin, need_ctx):
    p_l = h_l @ w_in
    p_c = h_c @ w_in
    bsz, L, _ = p_l.shape
    Lc = p_c.shape[1]

    def qkv(p, n):
        q = p[..., :Q_END].reshape(bsz, n, DA_HEADS, 2, DA_HEAD_DIM)
        k = p[..., Q_END:K_END].reshape(bsz, n, DA_HEADS, 2, DA_HEAD_DIM)
        v = p[..., K_END:V_END].reshape(bsz, n, DA_HEADS, DA_V_DIM)
        return q, k, v

    q_l, k_l, v_l = qkv(p_l, L)
    q_c, k_c, v_c = qkv(p_c, Lc)
    q_l = apply_rope(q_l, cos, sin)
    k_l = apply_rope(k_l, cos, sin)
    lv = lam_vec.astype(jnp.float32)
    lam = jnp.exp(jnp.sum(lv[0] * lv[1])) - jnp.exp(jnp.sum(lv[2] * lv[3])) + lam_init
    k_all = jnp.concatenate([k_c, k_l], axis=1)
    v_all = jnp.concatenate([v_c, v_l], axis=1)
    o_l = diff_attn_latent(q_l, k_all, v_all, lam)
    attn_l = (rmsnorm(o_l, sub_g) * (1.0 - lam_init)).reshape(bsz, L, DA_WIDTH)

    z_l, z_c = p_l[..., V_END:Z_END], p_c[..., V_END:Z_END]
    xbc_l = dwconv_centred(p_l[..., Z_END:XBC_END], conv_w, conv_b)
    xbc_c = dwconv_centred(p_c[..., Z_END:XBC_END], conv_w, conv_b)
    xl, bl, cl, dtl = ssd_inputs(xbc_l, p_l[..., XBC_END:], dt_bias)
    xc, bc, cc, dtc = ssd_inputs(xbc_c, p_c[..., XBC_END:], dt_bias)
    A = -jnp.exp(a_log.astype(jnp.float32))
    dsk = d_skip.astype(jnp.float32)[:, None]
    h0 = jnp.zeros((bsz, SSD_GROUPS, SSD_HEADS // SSD_GROUPS, SSD_HEAD_DIM, SSD_STATE),
                   jnp.float32)
    flip = lambda a: jnp.flip(a, axis=1)
    y_cf, h_cf = ssd_scan(xc, dtc[:, :, 0], A[0], bc, cc, h0)
    y_lf, _ = ssd_scan(xl, dtl[:, :, 0], A[0], bl, cl, h_cf)
    y_cb, h_cb = ssd_scan(flip(xc), flip(dtc[:, :, 1]), A[1], flip(bc), flip(cc), h0)
    y_lb, _ = ssd_scan(flip(xl), flip(dtl[:, :, 1]), A[1], flip(bl), flip(cl), h_cb)
    y_l = y_lf + flip(y_lb) + dsk * xl.astype(jnp.float32)
    ssd_l = gated_out_norm(y_l, z_l, ssd_g)

    mix_l = jnp.concatenate([attn_l, ssd_l], axis=-1) @ w_out
    if not need_ctx:
        return mix_l, None
    o_c = diff_attn_block(q_c, k_c, v_c, lam)
    attn_c = (rmsnorm(o_c, sub_g) * (1.0 - lam_init)).reshape(bsz, Lc, DA_WIDTH)
    y_c = y_cf + flip(y_cb) + dsk * xc.astype(jnp.float32)
    ssd_c = gated_out_norm(y_c, z_c, ssd_g)
    mix_c = jnp.concatenate([attn_c, ssd_c], axis=-1) @ w_out
    return mix_l, mix_c


def swiglu(h, w1, w2):
    gate, up = jnp.split(h @ w1, 2, axis=-1)
    return (jax.nn.silu(gate) * up) @ w2


def setup_inputs(seed: int = 0) -> dict:
    key = jax.random.key(seed)
    ks = jax.random.split(key, 20)
    f32 = jnp.float32
    s = D_MODEL ** -0.5
    dt0 = jnp.exp(jax.random.uniform(ks[9], (DEPTH, 2, SSD_HEADS), f32,
                                     math.log(1e-3), math.log(1e-1)))
    return {
        'x': jax.random.normal(ks[0], (BATCH, SEQ, D_MODEL), f32),
        'c': jax.random.normal(ks[1], (BATCH, D_MODEL), f32),
        'ctx': jax.random.normal(ks[2], (BATCH, CTX_LEN, D_MODEL), f32),
        'c_ctx': jax.random.normal(ks[3], (D_MODEL,), f32),
        'w_ada': jax.random.normal(ks[4], (DEPTH, D_MODEL, N_MOD * D_MODEL), f32) * (0.5 * s),
        'b_ada': jax.random.normal(ks[5], (DEPTH, N_MOD * D_MODEL), f32) * 0.01,
        'norm_g': 1.0 + 0.05 * jax.random.normal(ks[6], (DEPTH, 4, D_MODEL), f32),
        'w_in': jax.random.normal(ks[7], (DEPTH, D_MODEL, IN_COLS), f32) * s,
        'conv_w': jax.random.normal(ks[8], (DEPTH, SSD_CONV, SSD_CONV_CH), f32) * (SSD_CONV ** -0.5),
        'conv_b': jax.random.normal(ks[10], (DEPTH, SSD_CONV_CH), f32) * 0.01,
        'a_log': jnp.log(jax.random.uniform(ks[11], (DEPTH, 2, SSD_HEADS), f32, 1.0, 16.0)),
        'dt_bias': dt0 + jnp.log(-jnp.expm1(-dt0)),
        'd_skip': 1.0 + 0.05 * jax.random.normal(ks[12], (DEPTH, SSD_HEADS), f32),
        'ssd_norm_g': 1.0 + 0.05 * jax.random.normal(ks[13], (DEPTH, SSD_WIDTH), f32),
        'diff_lambda': 0.1 * jax.random.normal(ks[14], (DEPTH, 4, DA_HEAD_DIM), f32),
        'subln_g': 1.0 + 0.05 * jax.random.normal(ks[15], (DEPTH, DA_V_DIM), f32),
        'w_out': jax.random.normal(ks[16], (DEPTH, D_MIX, D_MODEL), f32) * (D_MIX ** -0.5),
        'w_ffn_in': jax.random.normal(ks[17], (DEPTH, D_MODEL, 2 * D_FF), f32) * s,
        'w_ffn_out': jax.random.normal(ks[18], (DEPTH, D_FF, D_MODEL), f32) * (D_FF ** -0.5),
    }


def reference(x, c, ctx, c_ctx, w_ada, b_ada, norm_g, w_in, conv_w, conv_b, a_log,
              dt_bias, d_skip, ssd_norm_g, diff_lambda, subln_g, w_out, w_ffn_in, w_ffn_out):
    L = x.shape[1]
    cos, sin = axial_rope_tables(L)
    sc = jax.nn.silu(c)
    scc = jax.nn.silu(c_ctx)
    for i in range(DEPTH):
        need_ctx = i < DEPTH - 1
        lam_init = 0.8 - 0.6 * math.exp(-0.3 * i)
        mod_l = (sc @ w_ada[i] + b_ada[i])[:, None, :]
        mod_c = (scc @ w_ada[i] + b_ada[i])[None, None, :]
        sh1, sc1, g1, sh2, sc2, g2 = jnp.split(mod_l, N_MOD, axis=-1)
        csh1, csc1, cg1, csh2, csc2, cg2 = jnp.split(mod_c, N_MOD, axis=-1)

        h_l = modulate(rmsnorm(x, norm_g[i, 0]), sh1, sc1)
        h_c = modulate(rmsnorm(ctx, norm_g[i, 0]), csh1, csc1)
        mix_l, mix_c = hybrid_mixer(h_l, h_c, w_in[i], conv_w[i], conv_b[i], a_log[i],
                                    dt_bias[i], d_skip[i], ssd_norm_g[i], diff_lambda[i],
                                    subln_g[i], w_out[i], lam_init, cos, sin, need_ctx)
        x = x + g1 * rmsnorm(mix_l, norm_g[i, 1])
        f_l = swiglu(modulate(rmsnorm(x, norm_g[i, 2]), sh2, sc2), w_ffn_in[i], w_ffn_out[i])
        x = x + g2 * rmsnorm(f_l, norm_g[i, 3])
        if need_ctx:
            ctx = ctx + cg1 * rmsnorm(mix_c, norm_g[i, 1])
            f_c = swiglu(modulate(rmsnorm(ctx, norm_g[i, 2]), csh2, csc2), w_ffn_in[i], w_ffn_out[i])
            ctx = ctx + cg2 * rmsnorm(f_c, norm_g[i, 3])
    return x
```

```python
import functools
import math

import jax
import jax.numpy as jnp
from jax import lax
from jax.experimental import pallas as pl
from jax.experimental.pallas import tpu as pltpu

F32 = jnp.float32
BF16 = jnp.bfloat16

D_MODEL = 1024
N_MOD = 6
CTX = 256
GRID_W = 64
DA_HEADS = 4
DA_HEAD_DIM = 64
DA_V_DIM = 2 * DA_HEAD_DIM
DA_WIDTH = DA_HEADS * DA_V_DIM
SSD_HEADS = 8
SSD_HEAD_DIM = 64
SSD_WIDTH = SSD_HEADS * SSD_HEAD_DIM
SSD_GROUPS = 2
SSD_STATE = 128
SSD_CONV = 5
SSD_CHUNK = 128
SSD_CONV_CH = SSD_WIDTH + 2 * SSD_GROUPS * SSD_STATE
D_FF = 2816
ROPE_BASE = 10000.0
EPS = 1e-6

LANES = 128
ROW_TILE = 640
CONV_TILE = 256
CONV_HALO = 16
ATT_TQ = 256
ATT_TK = 1024
FFN_CHUNK = 1408
VMEM_LIMIT = 56 * 1024 * 1024
NEG_BIG = -1e30
LOG2E = 1.4426950408889634


def _cparams(n_axes):
    return pltpu.CompilerParams(
        dimension_semantics=("arbitrary",) * n_axes, vmem_limit_bytes=VMEM_LIMIT)


def _rms(u, g):
    return u * lax.rsqrt(jnp.mean(u * u, axis=-1, keepdims=True) + EPS) * g


def _silu(u):
    return u * jax.nn.sigmoid(u)


def _row_is_ctx(step, tile):
    rows = step * tile + lax.broadcasted_iota(jnp.int32, (tile, 1), 0)
    return rows < CTX


def _const_spec(shape):
    zeros = (0,) * len(shape)
    return pl.BlockSpec(shape, lambda *_: zeros, pipeline_mode=pl.Buffered(1))


def _mod_kernel(c_ref, w_ref, b_ref, o_ref):
    s = _silu(c_ref[...])
    o_ref[...] = jnp.dot(s, w_ref[...], precision=lax.Precision.HIGHEST,
                         preferred_element_type=F32) + b_ref[...]


def _modulation(cond, w_ada, b_ada):
    depth = w_ada.shape[0]
    tn = 1536
    return pl.pallas_call(
        _mod_kernel,
        grid=(depth, N_MOD * D_MODEL // tn),
        in_specs=[pl.BlockSpec((8, D_MODEL), lambda i, j: (0, 0)),
                  pl.BlockSpec((None, D_MODEL, tn), lambda i, j: (i, 0, j)),
                  pl.BlockSpec((None, 1, tn), lambda i, j: (i, 0, j))],
        out_specs=pl.BlockSpec((None, 8, tn), lambda i, j: (i, 0, j)),
        out_shape=jax.ShapeDtypeStruct((depth, 8, N_MOD * D_MODEL), F32),
        compiler_params=_cparams(2),
        name="adaln_mod",
    )(cond, w_ada, b_ada.reshape(depth, 1, N_MOD * D_MODEL))


def _inproj_kernel(x_ref, g_ref, sh_ref, sc_ref, cos_ref, sin_ref, dtb_ref,
                   wq_ref, wk_ref, wv_ref, wz_ref, wx_ref, wdt_ref,
                   q_ref, k_ref, v_ref, z_ref, xbc_ref, dt_ref, dtt_ref):
    tile = x_ref.shape[0]
    is_ctx = _row_is_ctx(pl.program_id(1), tile)
    shift = jnp.where(is_ctx, sh_ref[1:2, :], sh_ref[0:1, :])
    scale = jnp.where(is_ctx, sc_ref[1:2, :], sc_ref[0:1, :])
    h = (_rms(x_ref[...], g_ref[...]) * (1.0 + scale) + shift).astype(BF16)

    cos = cos_ref[...]
    sin = sin_ref[...]
    lane = lax.broadcasted_iota(jnp.int32, (1, LANES), 1)
    first_half = (lane % DA_HEAD_DIM) < (DA_HEAD_DIM // 2)

    def rope(u):
        partner = jnp.where(first_half,
                            pltpu.roll(u, LANES - DA_HEAD_DIM // 2, axis=1),
                            pltpu.roll(u, DA_HEAD_DIM // 2, axis=1))
        return u * cos + partner * sin

    q = jnp.dot(h, wq_ref[...], preferred_element_type=F32)
    k = jnp.dot(h, wk_ref[...], preferred_element_type=F32)
    q_scale = DA_HEAD_DIM ** -0.5 * LOG2E
    for hd in range(DA_HEADS):
        sl = slice(hd * LANES, (hd + 1) * LANES)
        q_ref[:, sl] = (rope(q[:, sl]) * q_scale).astype(BF16)
        k_ref[:, sl] = rope(k[:, sl]).astype(BF16)
    v_ref[...] = jnp.dot(h, wv_ref[...], preferred_element_type=F32).astype(BF16)
    z_ref[...] = jnp.dot(h, wz_ref[...], preferred_element_type=F32).astype(BF16)
    xbc_ref[...] = jnp.dot(h, wx_ref[...], preferred_element_type=F32).astype(BF16)
    dt_raw = jnp.dot(h, wdt_ref[...], preferred_element_type=F32) + dtb_ref[...]
    dt = jnp.maximum(dt_raw, 0.0) + jnp.log1p(jnp.exp(-jnp.abs(dt_raw)))
    dt_ref[...] = dt
    dtt_ref[...] = dt.T[:2 * SSD_HEADS, :]


def _inproj(xx, g, sh, sc, cos_t, sin_t, dtb, wq, wk, wv, wz, wx, wdt):
    bsz, t, _ = xx.shape
    nt = t // ROW_TILE
    row = lambda w: pl.BlockSpec((None, ROW_TILE, w), lambda b, i: (b, i, 0))
    mod = pl.BlockSpec((None, 2, D_MODEL), lambda b, i: (b, 0, 0))
    tab = pl.BlockSpec((ROW_TILE, LANES), lambda b, i: (i, 0))
    return pl.pallas_call(
        _inproj_kernel,
        grid=(bsz, nt),
        in_specs=[row(D_MODEL), _const_spec((1, D_MODEL)), mod, mod, tab, tab,
                  _const_spec((1, LANES)),
                  _const_spec(wq.shape), _const_spec(wk.shape), _const_spec(wv.shape),
                  _const_spec(wz.shape), _const_spec(wx.shape), _const_spec(wdt.shape)],
        out_specs=[row(DA_WIDTH), row(DA_WIDTH), row(DA_WIDTH), row(SSD_WIDTH),
                   row(SSD_CONV_CH), row(LANES),
                   pl.BlockSpec((None, 2 * SSD_HEADS, ROW_TILE), lambda b, i: (b, 0, i))],
        out_shape=[jax.ShapeDtypeStruct((bsz, t, DA_WIDTH), BF16),
                   jax.ShapeDtypeStruct((bsz, t, DA_WIDTH), BF16),
                   jax.ShapeDtypeStruct((bsz, t, DA_WIDTH), BF16),
                   jax.ShapeDtypeStruct((bsz, t, SSD_WIDTH), BF16),
                   jax.ShapeDtypeStruct((bsz, t, SSD_CONV_CH), BF16),
                   jax.ShapeDtypeStruct((bsz, t, LANES), F32),
                   jax.ShapeDtypeStruct((bsz, 2 * SSD_HEADS, t), F32)],
        compiler_params=_cparams(2),
        name="inproj",
    )(xx, g, sh, sc, cos_t, sin_t, dtb, wq, wk, wv, wz, wx, wdt)


def _conv_kernel(cur_ref, prev_ref, next_ref, w_ref, b_ref, o_ref, ext_ref):
    i = pl.program_id(1)
    n = pl.num_programs(1)
    half = SSD_CONV // 2
    pad = 8
    first_of_segment = jnp.logical_or(i == 0, i == CTX // CONV_TILE)
    last_of_segment = jnp.logical_or(i == CTX // CONV_TILE - 1, i == n - 1)
    prev = prev_ref[...].astype(F32)[CONV_HALO - pad:, :]
    nxt = next_ref[...].astype(F32)[:pad, :]
    ext_ref[0:pad, :] = jnp.where(first_of_segment, 0.0, prev)
    ext_ref[pad:pad + CONV_TILE, :] = cur_ref[...].astype(F32)
    ext_ref[pad + CONV_TILE:, :] = jnp.where(last_of_segment, 0.0, nxt)
    acc = b_ref[...] + w_ref[half:half + 1, :] * ext_ref[pad:pad + CONV_TILE, :]
    for s in range(-half, half + 1):
        if s != 0:
            acc = acc + w_ref[half + s:half + s + 1, :] * ext_ref[pl.ds(pad + s, CONV_TILE), :]
    o_ref[...] = _silu(acc).astype(BF16)


def _conv(xbc, w, b):
    bsz, t, ch = xbc.shape
    nt = t // CONV_TILE
    per = CONV_TILE // CONV_HALO
    last_halo = t // CONV_HALO - 1
    return pl.pallas_call(
        _conv_kernel,
        grid=(bsz, nt),
        in_specs=[pl.BlockSpec((None, CONV_TILE, ch), lambda b_, i: (b_, i, 0)),
                  pl.BlockSpec((None, CONV_HALO, ch),
                               lambda b_, i: (b_, jnp.maximum(i * per - 1, 0), 0)),
                  pl.BlockSpec((None, CONV_HALO, ch),
                               lambda b_, i: (b_, jnp.minimum((i + 1) * per, last_halo), 0)),
                  _const_spec((8, ch)), _const_spec((1, ch))],
        out_specs=pl.BlockSpec((None, CONV_TILE, ch), lambda b_, i: (b_, i, 0)),
        out_shape=jax.ShapeDtypeStruct((bsz, t, ch), BF16),
        scratch_shapes=[pltpu.VMEM((CONV_TILE + 16, ch), F32)],
        compiler_params=_cparams(2),
        name="dwconv",
    )(xbc, xbc, xbc, w, b)


def _ssd_direction(d, xc_ref, dt_ref, dtt_ref, arow_ref, acol_ref, s_ref, y_ref):
    q = SSD_CHUNK
    xall = xc_ref[...]
    dt = dt_ref[...]
    dtt = dtt_ref[...]
    da = dt * (-jnp.exp(arow_ref[...]))
    dat = dtt * (-jnp.exp(acol_ref[...]))
    ii = lax.broadcasted_iota(jnp.int32, (q, q), 0)
    jj = lax.broadcasted_iota(jnp.int32, (q, q), 1)
    mask = (jj <= ii) if d == 0 else (jj >= ii)
    mf = mask.astype(F32)
    hi = lax.Precision.HIGHEST
    cum_col = jnp.dot(mf, da, precision=hi, preferred_element_type=F32)
    cum_row = lax.dot_general(dat, mf, (((1,), (1,)), ((), ())), precision=hi,
                              preferred_element_type=F32)
    tot = jnp.sum(dat, axis=1, keepdims=True)
    lane = lax.broadcasted_iota(jnp.int32, (1, LANES), 1)
    low_half = lane < SSD_HEAD_DIM
    heads_per_group = SSD_HEADS // SSD_GROUPS
    for g in range(SSD_GROUPS):
        bg = xall[:, SSD_WIDTH + g * SSD_STATE:SSD_WIDTH + (g + 1) * SSD_STATE]
        c0 = SSD_WIDTH + SSD_GROUPS * SSD_STATE + g * SSD_STATE
        cg = xall[:, c0:c0 + SSD_STATE]
        cb = lax.dot_general(cg, bg, (((1,), (1,)), ((), ())), preferred_element_type=F32)
        bgt = bg.astype(F32).T
        cgf = cg.astype(F32)
        for pr in range(heads_per_group // 2):
            pair = g * (heads_per_group // 2) + pr
            xp = xall[:, pair * LANES:(pair + 1) * LANES]
            sp = s_ref[pair]
            spb = sp.astype(BF16)
            ys, us, decs = [], [], []
            for r in range(2):
                ln = 8 * d + 2 * pair + r
                colb = cum_col[:, ln:ln + 1]
                rowb = cum_row[ln:ln + 1, :]
                dtr = dtt[ln:ln + 1, :]
                tot_h = tot[ln:ln + 1, :]
                lmat = jnp.exp(jnp.where(mask, colb - rowb, -jnp.inf))
                w = (cb * lmat * dtr).astype(BF16)
                e = (cgf * jnp.exp(colb)).astype(BF16)
                ys.append(jnp.dot(w, xp, preferred_element_type=F32)
                          + jnp.dot(e, spb, preferred_element_type=F32))
                wrow = dtr * jnp.exp(tot_h - rowb)
                us.append(jnp.dot((bgt * wrow).astype(BF16), xp, preferred_element_type=F32))
                decs.append(jnp.exp(tot_h))
            y_ref[:, pair * LANES:(pair + 1) * LANES] = jnp.where(low_half, ys[0], ys[1])
            dec = jnp.where(low_half, decs[0], decs[1])
            s_ref[pair] = sp * dec + jnp.where(low_half, us[0], us[1])


def _ssd_kernel(xf_ref, dtf_ref, dttf_ref, xb_ref, dtb_ref, dttb_ref, arow_ref, acol_ref,
                yf_ref, yb_ref, s_ref):
    @pl.when(pl.program_id(1) == 0)
    def _():
        s_ref[...] = jnp.zeros_like(s_ref)

    _ssd_direction(0, xf_ref, dtf_ref, dttf_ref, arow_ref, acol_ref, s_ref.at[0], yf_ref)
    _ssd_direction(1, xb_ref, dtb_ref, dttb_ref, arow_ref, acol_ref, s_ref.at[1], yb_ref)


def _ssd(xc, dt, dtt, arow, acol):
    bsz, t, ch = xc.shape
    nc = t // SSD_CHUNK
    nctx = CTX // SSD_CHUNK

    def fwd(c):
        return c

    def bwd(c):
        return jnp.where(c < nctx, nctx - 1 - c, nc - 1 + nctx - c)

    def specs(order):
        return [pl.BlockSpec((None, SSD_CHUNK, ch), lambda b, c: (b, order(c), 0)),
                pl.BlockSpec((None, SSD_CHUNK, LANES), lambda b, c: (b, order(c), 0)),
                pl.BlockSpec((None, 2 * SSD_HEADS, SSD_CHUNK), lambda b, c: (b, 0, order(c)))]

    y_spec = lambda order: pl.BlockSpec((None, SSD_CHUNK, SSD_WIDTH),
                                        lambda b, c: (b, order(c), 0))
    y_shape = jax.ShapeDtypeStruct((bsz, t, SSD_WIDTH), F32)
    return pl.pallas_call(
        _ssd_kernel,
        grid=(bsz, nc),
        in_specs=specs(fwd) + specs(bwd) + [_const_spec((1, LANES)),
                                            _const_spec((2 * SSD_HEADS, 1))],
        out_specs=[y_spec(fwd), y_spec(bwd)],
        out_shape=[y_shape, y_shape],
        scratch_shapes=[pltpu.VMEM((2, SSD_HEADS // 2, SSD_STATE, LANES), F32)],
        compiler_params=_cparams(2),
        name="ssd_scan",
    )(xc, dt, dtt, xc, dt, dtt, arow, acol)


def _attn_kernel(lam_init, q_ref, k_ref, v_ref, lv_ref, sg_ref, o_ref, vext_ref, acc_ref, m_ref):
    qi = pl.program_id(2)
    t = k_ref.shape[0]
    n_lat = (t - CTX) // ATT_TK

    @pl.when(qi == 0)
    def _():
        vext_ref[:, :LANES] = v_ref[...]
        vext_ref[:, LANES:] = jnp.ones((t, LANES), BF16)

    q = q_ref[...]
    lane = lax.broadcasted_iota(jnp.int32, (1, LANES), 1)
    comp0 = lane < DA_HEAD_DIM
    zero = jnp.zeros_like(q)
    qm = (jnp.where(comp0, q, zero), jnp.where(comp0, zero, q))
    m_ref[...] = jnp.full(m_ref.shape, NEG_BIG, F32)
    acc_ref[...] = jnp.zeros_like(acc_ref)

    def chunk(start, size):
        kc = k_ref[pl.ds(start, size), :]
        vc = vext_ref[pl.ds(start, size), :]
        for c in range(2):
            s = lax.dot_general(qm[c], kc, (((1,), (1,)), ((), ())),
                                preferred_element_type=F32)
            m_prev = m_ref[c]
            m_next = jnp.maximum(m_prev, jnp.max(s, axis=1, keepdims=True))
            alpha = jnp.exp2(m_prev - m_next)
            p = jnp.exp2(s - m_next).astype(BF16)
            pv = jnp.dot(p, vc, preferred_element_type=F32)
            acc_ref[c] = acc_ref[c] * alpha + pv
            m_ref[c] = m_next

    chunk(0, CTX)

    def body(j, carry):
        chunk(pl.multiple_of(CTX + j * ATT_TK, ATT_TK // 4), ATT_TK)
        return carry

    lax.fori_loop(0, jnp.where(qi == 0, 0, n_lat), body, 0)

    lv = lv_ref[...]
    lam = (jnp.exp(jnp.sum(lv[0:1] * lv[1:2], axis=1, keepdims=True))
           - jnp.exp(jnp.sum(lv[2:3] * lv[3:4], axis=1, keepdims=True)) + lam_init)
    a0 = acc_ref[0]
    a1 = acc_ref[1]
    o = a0[:, :LANES] / a0[:, LANES:] - lam * (a1[:, :LANES] / a1[:, LANES:])
    o_ref[...] = (_rms(o, sg_ref[...]) * (1.0 - lam_init)).astype(BF16)


def _attention(q, k, v, lam_vec, sub_g, lam_init):
    bsz, t, _ = q.shape
    nq = t // ATT_TQ
    kv = pl.BlockSpec((None, t, LANES), lambda b, h, i: (b, 0, h))
    qo = pl.BlockSpec((None, ATT_TQ, LANES), lambda b, h, i: (b, i, h))
    return pl.pallas_call(
        functools.partial(_attn_kernel, lam_init),
        grid=(bsz, DA_HEADS, nq),
        in_specs=[qo, kv, kv, _const_spec((4, DA_HEAD_DIM)), _const_spec((1, DA_V_DIM))],
        out_specs=qo,
        out_shape=jax.ShapeDtypeStruct((bsz, t, DA_WIDTH), BF16),
        scratch_shapes=[pltpu.VMEM((t, 2 * LANES), BF16),
                        pltpu.VMEM((2, ATT_TQ, 2 * LANES), F32),
                        pltpu.VMEM((2, ATT_TQ, 1), F32)],
        compiler_params=_cparams(3),
        name="diff_attn",
    )(q, k, v, lam_vec, sub_g)


def _mixout_kernel(x_ref, at_ref, yf_ref, yb_ref, xc_ref, z_ref, dsk_ref, sg_ref, g_ref,
                   gate_ref, wa_ref, ws_ref, o_ref):
    tile = x_ref.shape[0]
    is_ctx = _row_is_ctx(pl.program_id(1), tile)
    gate = jnp.where(is_ctx, gate_ref[1:2, :], gate_ref[0:1, :])
    y = yf_ref[...] + yb_ref[...] + dsk_ref[...] * xc_ref[...].astype(F32)
    u = y * _silu(z_ref[...].astype(F32))
    ssd = _rms(u, sg_ref[...]).astype(BF16)
    mix = (jnp.dot(at_ref[...], wa_ref[...], preferred_element_type=F32)
           + jnp.dot(ssd, ws_ref[...], preferred_element_type=F32))
    o_ref[...] = x_ref[...] + gate * _rms(mix, g_ref[...])


def _mixout(xx, attn, yf, yb, xc, z, dsk, ssd_g, g, gate, wa, ws):
    bsz, t, _ = xx.shape
    nt = t // ROW_TILE
    row = lambda w: pl.BlockSpec((None, ROW_TILE, w), lambda b, i: (b, i, 0))
    mod = pl.BlockSpec((None, 2, D_MODEL), lambda b, i: (b, 0, 0))
    return pl.pallas_call(
        _mixout_kernel,
        grid=(bsz, nt),
        in_specs=[row(D_MODEL), row(DA_WIDTH), row(SSD_WIDTH), row(SSD_WIDTH),
                  row(SSD_WIDTH),
                  row(SSD_WIDTH), _const_spec((1, SSD_WIDTH)), _const_spec((1, SSD_WIDTH)),
                  _const_spec((1, D_MODEL)), mod, _const_spec(wa.shape), _const_spec(ws.shape)],
        out_specs=row(D_MODEL),
        out_shape=jax.ShapeDtypeStruct(xx.shape, F32),
        compiler_params=_cparams(2),
        name="mix_out",
    )(xx, attn, yf, yb, xc, z, dsk, ssd_g, g, gate, wa, ws)


def _ffn_kernel(x_ref, gin_ref, sh_ref, sc_ref, gout_ref, gate_ref, wg_ref, wu_ref, wo_ref,
                o_ref):
    tile = x_ref.shape[0]
    is_ctx = _row_is_ctx(pl.program_id(1), tile)
    pick = lambda r: jnp.where(is_ctx, r[1:2, :], r[0:1, :])
    x = x_ref[...]
    h = (_rms(x, gin_ref[...]) * (1.0 + pick(sc_ref)) + pick(sh_ref)).astype(BF16)
    acc = None
    for c in range(D_FF // FFN_CHUNK):
        sl = slice(c * FFN_CHUNK, (c + 1) * FFN_CHUNK)
        gt = jnp.dot(h, wg_ref[:, sl], preferred_element_type=F32)
        up = jnp.dot(h, wu_ref[:, sl], preferred_element_type=F32)
        a = (_silu(gt) * up).astype(BF16)
        part = jnp.dot(a, wo_ref[sl, :], preferred_element_type=F32)
        acc = part if acc is None else acc + part
    o_ref[...] = x + pick(gate_ref) * _rms(acc, gout_ref[...])


def _ffn(xx, gin, sh, sc, gout, gate, wg, wu, wo):
    bsz, t, _ = xx.shape
    nt = t // ROW_TILE
    row = pl.BlockSpec((None, ROW_TILE, D_MODEL), lambda b, i: (b, i, 0))
    mod = pl.BlockSpec((None, 2, D_MODEL), lambda b, i: (b, 0, 0))
    return pl.pallas_call(
        _ffn_kernel,
        grid=(bsz, nt),
        in_specs=[row, _const_spec((1, D_MODEL)), mod, mod, _const_spec((1, D_MODEL)), mod,
                  _const_spec(wg.shape), _const_spec(wu.shape), _const_spec(wo.shape)],
        out_specs=row,
        out_shape=jax.ShapeDtypeStruct(xx.shape, F32),
        compiler_params=_cparams(2),
        name="ffn",
    )(xx, gin, sh, sc, gout, gate, wg, wu, wo)


def _rope_tables(seq):
    rows = seq // GRID_W
    row = jnp.repeat(jnp.arange(rows, dtype=F32), GRID_W)
    col = jnp.tile(jnp.arange(GRID_W, dtype=F32), rows)
    nf = DA_HEAD_DIM // 4
    inv = ROPE_BASE ** (-jnp.arange(nf, dtype=F32) / nf)
    ang = jnp.concatenate([row[:, None] * inv, col[:, None] * inv], axis=-1)
    cos = jnp.concatenate([jnp.ones((CTX, DA_HEAD_DIM // 2), F32), jnp.cos(ang)], axis=0)
    sin = jnp.concatenate([jnp.zeros((CTX, DA_HEAD_DIM // 2), F32), jnp.sin(ang)], axis=0)
    return jnp.tile(cos, (1, 4)), jnp.tile(jnp.concatenate([-sin, sin], axis=1), (1, 2))


def kernel(x, c, ctx, c_ctx, w_ada, b_ada, norm_g, w_in, conv_w, conv_b, a_log, dt_bias,
           d_skip, ssd_norm_g, diff_lambda, subln_g, w_out, w_ffn_in, w_ffn_out):
    bsz, seq, _ = x.shape
    depth = w_ada.shape[0]
    t = CTX + seq
    assert ctx.shape[1] == CTX and t % ROW_TILE == 0 and seq % ATT_TK == 0

    xx = jnp.concatenate([ctx, x], axis=1)
    cond = jnp.zeros((8, D_MODEL), F32).at[:bsz].set(c).at[bsz].set(c_ctx)
    mod = _modulation(cond, w_ada, b_ada)
    cos_t, sin_t = _rope_tables(seq)

    q_end = DA_WIDTH
    k_end = 2 * DA_WIDTH
    v_end = 3 * DA_WIDTH
    z_end = v_end + SSD_WIDTH
    xbc_end = z_end + SSD_CONV_CH

    for i in range(depth):
        lam_init = 0.8 - 0.6 * math.exp(-0.3 * i)
        mods = [jnp.stack([mod[i, :bsz, j * D_MODEL:(j + 1) * D_MODEL],
                           jnp.broadcast_to(mod[i, bsz, j * D_MODEL:(j + 1) * D_MODEL],
                                            (bsz, D_MODEL))], axis=1) for j in range(N_MOD)]
        sh1, sc1, g1, sh2, sc2, g2 = mods
        wi = w_in[i].astype(BF16)
        wdt = jnp.zeros((D_MODEL, LANES), BF16).at[:, :2 * SSD_HEADS].set(wi[:, xbc_end:])
        lane16 = lambda a: jnp.zeros((1, LANES), F32).at[0, :2 * SSD_HEADS].set(a.reshape(-1))
        q, k, v, z, xbc, dt, dtt = _inproj(
            xx, norm_g[i, 0][None], sh1, sc1, cos_t, sin_t, lane16(dt_bias[i]),
            wi[:, :q_end], wi[:, q_end:k_end], wi[:, k_end:v_end], wi[:, v_end:z_end],
            wi[:, z_end:xbc_end], wdt)
        cw = jnp.zeros((8, SSD_CONV_CH), F32).at[:SSD_CONV].set(conv_w[i])
        xc = _conv(xbc, cw, conv_b[i][None])
        yf, yb = _ssd(xc, dt, dtt, lane16(a_log[i]), a_log[i].reshape(2 * SSD_HEADS, 1))
        attn = _attention(q, k, v, diff_lambda[i], subln_g[i][None], lam_init)
        wo = w_out[i].astype(BF16)
        xx = _mixout(xx, attn, yf, yb, xc, z, jnp.repeat(d_skip[i], SSD_HEAD_DIM)[None],
                     ssd_norm_g[i][None], norm_g[i, 1][None], g1, wo[:DA_WIDTH], wo[DA_WIDTH:])
        wf = w_ffn_in[i].astype(BF16)
        xx = _ffn(xx, norm_g[i, 2][None], sh2, sc2, norm_g[i, 3][None], g2,
                  wf[:, :D_FF], wf[:, D_FF:], w_ffn_out[i].astype(BF16))
    return xx[:, CTX:]
```

```python
import functools
import math

import jax
import jax.numpy as jnp
from jax import lax
from jax.experimental import pallas as pl
from jax.experimental.pallas import tpu as pltpu

F32 = jnp.float32
BF16 = jnp.bfloat16

D_MODEL = 1024
N_MOD = 6
CTX = 256
GRID_W = 64
DA_HEADS = 4
DA_HEAD_DIM = 64
DA_V_DIM = 2 * DA_HEAD_DIM
DA_WIDTH = DA_HEADS * DA_V_DIM
SSD_HEADS = 8
SSD_HEAD_DIM = 64
SSD_WIDTH = SSD_HEADS * SSD_HEAD_DIM
SSD_GROUPS = 2
SSD_STATE = 128
SSD_CONV = 5
SSD_CHUNK = 128
SSD_CONV_CH = SSD_WIDTH + 2 * SSD_GROUPS * SSD_STATE
D_FF = 2816
ROPE_BASE = 10000.0
EPS = 1e-6

LANES = 128
ROW_TILE = 640
CONV_TILE = 256
CONV_HALO = 16
ATT_TQ = 256
ATT_TK = 1024
FFN_CHUNK = 1408
VMEM_LIMIT = 56 * 1024 * 1024
NEG_BIG = -1e30
LOG2E = 1.4426950408889634


def _cparams(n_axes):
    return pltpu.CompilerParams(
        dimension_semantics=("arbitrary",) * n_axes, vmem_limit_bytes=VMEM_LIMIT)


def _rms(u, g):
    return u * lax.rsqrt(jnp.mean(u * u, axis=-1, keepdims=True) + EPS) * g


def _silu(u):
    return u * jax.nn.sigmoid(u)


def _row_is_ctx(step, tile):
    rows = step * tile + lax.broadcasted_iota(jnp.int32, (tile, 1), 0)
    return rows < CTX


def _const_spec(shape):
    zeros = (0,) * len(shape)
    return pl.BlockSpec(shape, lambda *_: zeros, pipeline_mode=pl.Buffered(1))


def _mod_kernel(c_ref, w_ref, b_ref, o_ref):
    s = _silu(c_ref[...])
    o_ref[...] = jnp.dot(s, w_ref[...], precision=lax.Precision.HIGHEST,
                         preferred_element_type=F32) + b_ref[...]


def _modulation(cond, w_ada, b_ada):
    depth = w_ada.shape[0]
    tn = 1536
    return pl.pallas_call(
        _mod_kernel,
        grid=(depth, N_MOD * D_MODEL // tn),
        in_specs=[pl.BlockSpec((8, D_MODEL), lambda i, j: (0, 0)),
                  pl.BlockSpec((None, D_MODEL, tn), lambda i, j: (i, 0, j)),
                  pl.BlockSpec((None, 1, tn), lambda i, j: (i, 0, j))],
        out_specs=pl.BlockSpec((None, 8, tn), lambda i, j: (i, 0, j)),
        out_shape=jax.ShapeDtypeStruct((depth, 8, N_MOD * D_MODEL), F32),
        compiler_params=_cparams(2),
        name="adaln_mod",
    )(cond, w_ada, b_ada.reshape(depth, 1, N_MOD * D_MODEL))


def _inproj_kernel(x_ref, g_ref, sh_ref, sc_ref, cos_ref, sin_ref, dtb_ref,
                   wq_ref, wk_ref, wv_ref, wz_ref, wx_ref, wdt_ref,
                   q_ref, k_ref, v_ref, z_ref, xbc_ref, dt_ref, dtt_ref):
    tile = x_ref.shape[0]
    is_ctx = _row_is_ctx(pl.program_id(1), tile)
    shift = jnp.where(is_ctx, sh_ref[1:2, :], sh_ref[0:1, :])
    scale = jnp.where(is_ctx, sc_ref[1:2, :], sc_ref[0:1, :])
    h = (_rms(x_ref[...], g_ref[...]) * (1.0 + scale) + shift).astype(BF16)

    cos = cos_ref[...]
    sin = sin_ref[...]
    lane = lax.broadcasted_iota(jnp.int32, (1, LANES), 1)
    first_half = (lane % DA_HEAD_DIM) < (DA_HEAD_DIM // 2)

    def rope(u):
        partner = jnp.where(first_half,
                            pltpu.roll(u, LANES - DA_HEAD_DIM // 2, axis=1),
                            pltpu.roll(u, DA_HEAD_DIM // 2, axis=1))
        return u * cos + partner * sin

    q = jnp.dot(h, wq_ref[...], preferred_element_type=F32)
    k = jnp.dot(h, wk_ref[...], preferred_element_type=F32)
    q_scale = DA_HEAD_DIM ** -0.5 * LOG2E
    for hd in range(DA_HEADS):
        sl = slice(hd * LANES, (hd + 1) * LANES)
        q_ref[:, sl] = (rope(q[:, sl]) * q_scale).astype(BF16)
        k_ref[:, sl] = rope(k[:, sl]).astype(BF16)
    v_ref[...] = jnp.dot(h, wv_ref[...], preferred_element_type=F32).astype(BF16)
    z_ref[...] = jnp.dot(h, wz_ref[...], preferred_element_type=F32).astype(BF16)
    xbc_ref[...] = jnp.dot(h, wx_ref[...], preferred_element_type=F32).astype(BF16)
    dt_raw = jnp.dot(h, wdt_ref[...], preferred_element_type=F32) + dtb_ref[...]
    dt = jnp.maximum(dt_raw, 0.0) + jnp.log1p(jnp.exp(-jnp.abs(dt_raw)))
    dt_ref[...] = dt
    dtt_ref[...] = dt.T[:2 * SSD_HEADS, :]


def _inproj(xx, g, sh, sc, cos_t, sin_t, dtb, wq, wk, wv, wz, wx, wdt):
    bsz, t, _ = xx.shape
    nt = t // ROW_TILE
    row = lambda w: pl.BlockSpec((None, ROW_TILE, w), lambda b, i: (b, i, 0))
    mod = pl.BlockSpec((None, 2, D_MODEL), lambda b, i: (b, 0, 0))
    tab = pl.BlockSpec((ROW_TILE, LANES), lambda b, i: (i, 0))
    return pl.pallas_call(
        _inproj_kernel,
        grid=(bsz, nt),
        in_specs=[row(D_MODEL), _const_spec((1, D_MODEL)), mod, mod, tab, tab,
                  _const_spec((1, LANES)),
                  _const_spec(wq.shape), _const_spec(wk.shape), _const_spec(wv.shape),
                  _const_spec(wz.shape), _const_spec(wx.shape), _const_spec(wdt.shape)],
        out_specs=[row(DA_WIDTH), row(DA_WIDTH), row(DA_WIDTH), row(SSD_WIDTH),
                   row(SSD_CONV_CH), row(LANES),
                   pl.BlockSpec((None, 2 * SSD_HEADS, ROW_TILE), lambda b, i: (b, 0, i))],
        out_shape=[jax.ShapeDtypeStruct((bsz, t, DA_WIDTH), BF16),
                   jax.ShapeDtypeStruct((bsz, t, DA_WIDTH), BF16),
                   jax.ShapeDtypeStruct((bsz, t, DA_WIDTH), BF16),
                   jax.ShapeDtypeStruct((bsz, t, SSD_WIDTH), BF16),
                   jax.ShapeDtypeStruct((bsz, t, SSD_CONV_CH), BF16),
                   jax.ShapeDtypeStruct((bsz, t, LANES), F32),
                   jax.ShapeDtypeStruct((bsz, 2 * SSD_HEADS, t), F32)],
        compiler_params=_cparams(2),
        name="inproj",
    )(xx, g, sh, sc, cos_t, sin_t, dtb, wq, wk, wv, wz, wx, wdt)


def _conv_kernel(cur_ref, prev_ref, next_ref, w_ref, b_ref, o_ref, ext_ref):
    i = pl.program_id(1)
    n = pl.num_programs(1)
    half = SSD_CONV // 2
    pad = 8
    first_of_segment = jnp.logical_or(i == 0, i == CTX // CONV_TILE)
    last_of_segment = jnp.logical_or(i == CTX // CONV_TILE - 1, i == n - 1)
    prev = prev_ref[...].astype(F32)[CONV_HALO - pad:, :]
    nxt = next_ref[...].astype(F32)[:pad, :]
    ext_ref[0:pad, :] = jnp.where(first_of_segment, 0.0, prev)
    ext_ref[pad:pad + CONV_TILE, :] = cur_ref[...].astype(F32)
    ext_ref[pad + CONV_TILE:, :] = jnp.where(last_of_segment, 0.0, nxt)
    acc = b_ref[...] + w_ref[half:half + 1, :] * ext_ref[pad:pad + CONV_TILE, :]
    for s in range(-half, half + 1):
        if s != 0:
            acc = acc + w_ref[half + s:half + s + 1, :] * ext_ref[pl.ds(pad + s, CONV_TILE), :]
    o_ref[...] = _silu(acc).astype(BF16)


def _conv(xbc, w, b):
    bsz, t, ch = xbc.shape
    nt = t // CONV_TILE
    per = CONV_TILE // CONV_HALO
    last_halo = t // CONV_HALO - 1
    return pl.pallas_call(
        _conv_kernel,
        grid=(bsz, nt),
        in_specs=[pl.BlockSpec((None, CONV_TILE, ch), lambda b_, i: (b_, i, 0)),
                  pl.BlockSpec((None, CONV_HALO, ch),
                               lambda b_, i: (b_, jnp.maximum(i * per - 1, 0), 0)),
                  pl.BlockSpec((None, CONV_HALO, ch),
                               lambda b_, i: (b_, jnp.minimum((i + 1) * per, last_halo), 0)),
                  _const_spec((8, ch)), _const_spec((1, ch))],
        out_specs=pl.BlockSpec((None, CONV_TILE, ch), lambda b_, i: (b_, i, 0)),
        out_shape=jax.ShapeDtypeStruct((bsz, t, ch), BF16),
        scratch_shapes=[pltpu.VMEM((CONV_TILE + 16, ch), F32)],
        compiler_params=_cparams(2),
        name="dwconv",
    )(xbc, xbc, xbc, w, b)


def _ssd_direction(d, xc_ref, dt_ref, dtt_ref, arow_ref, acol_ref, s_ref, y_ref):
    q = SSD_CHUNK
    xall = xc_ref[...]
    dt = dt_ref[...]
    dtt = dtt_ref[...]
    da = dt * (-jnp.exp(arow_ref[...]))
    dat = dtt * (-jnp.exp(acol_ref[...]))
    ii = lax.broadcasted_iota(jnp.int32, (q, q), 0)
    jj = lax.broadcasted_iota(jnp.int32, (q, q), 1)
    mask = (jj <= ii) if d == 0 else (jj >= ii)
    mf = mask.astype(F32)
    hi = lax.Precision.HIGHEST
    cum_col = jnp.dot(mf, da, precision=hi, preferred_element_type=F32)
    cum_row = lax.dot_general(dat, mf, (((1,), (1,)), ((), ())), precision=hi,
                              preferred_element_type=F32)
    tot = jnp.sum(dat, axis=1, keepdims=True)
    lane = lax.broadcasted_iota(jnp.int32, (1, LANES), 1)
    low_half = lane < SSD_HEAD_DIM
    heads_per_group = SSD_HEADS // SSD_GROUPS
    for g in range(SSD_GROUPS):
        bg = xall[:, SSD_WIDTH + g * SSD_STATE:SSD_WIDTH + (g + 1) * SSD_STATE]
        c0 = SSD_WIDTH + SSD_GROUPS * SSD_STATE + g * SSD_STATE
        cg = xall[:, c0:c0 + SSD_STATE]
        cb = lax.dot_general(cg, bg, (((1,), (1,)), ((), ())), preferred_element_type=F32)
        bgt = bg.astype(F32).T
        cgf = cg.astype(F32)
        for pr in range(heads_per_group // 2):
            pair = g * (heads_per_group // 2) + pr
            xp = xall[:, pair * LANES:(pair + 1) * LANES]
            sp = s_ref[pair]
            spb = sp.astype(BF16)
            ys, us, decs = [], [], []
            for r in range(2):
                ln = 8 * d + 2 * pair + r
                colb = cum_col[:, ln:ln + 1]
                rowb = cum_row[ln:ln + 1, :]
                dtr = dtt[ln:ln + 1, :]
                tot_h = tot[ln:ln + 1, :]
                lmat = jnp.exp(jnp.where(mask, colb - rowb, -jnp.inf))
                w = (cb * lmat * dtr).astype(BF16)
                e = (cgf * jnp.exp(colb)).astype(BF16)
                ys.append(jnp.dot(w, xp, preferred_element_type=F32)
                          + jnp.dot(e, spb, preferred_element_type=F32))
                wrow = dtr * jnp.exp(tot_h - rowb)
                us.append(jnp.dot((bgt * wrow).astype(BF16), xp, preferred_element_type=F32))
                decs.append(jnp.exp(tot_h))
            y_ref[:, pair * LANES:(pair + 1) * LANES] = jnp.where(low_half, ys[0], ys[1])
            dec = jnp.where(low_half, decs[0], decs[1])
            s_ref[pair] = sp * dec + jnp.where(low_half, us[0], us[1])


def _ssd_kernel(xf_ref, dtf_ref, dttf_ref, xb_ref, dtb_ref, dttb_ref, arow_ref, acol_ref,
                yf_ref, yb_ref, s_ref):
    @pl.when(pl.program_id(1) == 0)
    def _():
        s_ref[...] = jnp.zeros_like(s_ref)

    _ssd_direction(0, xf_ref, dtf_ref, dttf_ref, arow_ref, acol_ref, s_ref.at[0], yf_ref)
    _ssd_direction(1, xb_ref, dtb_ref, dttb_ref, arow_ref, acol_ref, s_ref.at[1], yb_ref)


def _ssd(xc, dt, dtt, arow, acol):
    bsz, t, ch = xc.shape
    nc = t // SSD_CHUNK
    nctx = CTX // SSD_CHUNK

    def fwd(c):
        return c

    def bwd(c):
        return jnp.where(c < nctx, nctx - 1 - c, nc - 1 + nctx - c)

    def specs(order):
        return [pl.BlockSpec((None, SSD_CHUNK, ch), lambda b, c: (b, order(c), 0)),
                pl.BlockSpec((None, SSD_CHUNK, LANES), lambda b, c: (b, order(c), 0)),
                pl.BlockSpec((None, 2 * SSD_HEADS, SSD_CHUNK), lambda b, c: (b, 0, order(c)))]

    y_spec = lambda order: pl.BlockSpec((None, SSD_CHUNK, SSD_WIDTH),
                                        lambda b, c: (b, order(c), 0))
    y_shape = jax.ShapeDtypeStruct((bsz, t, SSD_WIDTH), F32)
    return pl.pallas_call(
        _ssd_kernel,
        grid=(bsz, nc),
        in_specs=specs(fwd) + specs(bwd) + [_const_spec((1, LANES)),
                                            _const_spec((2 * SSD_HEADS, 1))],
        out_specs=[y_spec(fwd), y_spec(bwd)],
        out_shape=[y_shape, y_shape],
        scratch_shapes=[pltpu.VMEM((2, SSD_HEADS // 2, SSD_STATE, LANES), F32)],
        compiler_params=_cparams(2),
        name="ssd_scan",
    )(xc, dt, dtt, xc, dt, dtt, arow, acol)


def _attn_kernel(lam_init, q_ref, k_ref, v_ref, lv_ref, sg_ref, o_ref, vext_ref, acc_ref, m_ref,
                 qm_ref, s0_ref, s1_ref, p0_ref, p1_ref, al0_ref, al1_ref):
    qi = pl.program_id(2)
    t = k_ref.shape[0]
    n_lat = (t - CTX) // ATT_TK
    assert n_lat >= 2 and n_lat % 2 == 0
    s_bufs, p_bufs, al_bufs = (s0_ref, s1_ref), (p0_ref, p1_ref), (al0_ref, al1_ref)
    nt_dims = (((1,), (1,)), ((), ()))

    @pl.when(qi == 0)
    def _():
        vext_ref[:, :LANES] = v_ref[...]
        vext_ref[:, LANES:] = jnp.ones((t, LANES), BF16)

    q = q_ref[...]
    lane = lax.broadcasted_iota(jnp.int32, (1, LANES), 1)
    comp0 = lane < DA_HEAD_DIM
    zero = jnp.zeros_like(q)
    qm_ref[0] = jnp.where(comp0, q, zero)
    qm_ref[1] = jnp.where(comp0, zero, q)

    kc = k_ref[0:CTX, :]
    vc = vext_ref[0:CTX, :]
    for c in range(2):
        s = lax.dot_general(qm_ref[c], kc, nt_dims, preferred_element_type=F32)
        m_next = jnp.max(s, axis=1, keepdims=True)
        p = jnp.exp2(s - m_next).astype(BF16)
        acc_ref[c] = jnp.dot(p, vc, preferred_element_type=F32)
        m_ref[c] = m_next

    def scores(j, slot):
        kc = k_ref[pl.ds(pl.multiple_of(CTX + j * ATT_TK, CTX), ATT_TK), :]
        for c in range(2):
            s_bufs[slot][c] = lax.dot_general(qm_ref[c], kc, nt_dims,
                                              preferred_element_type=F32)

    def softmax(slot):
        for c in range(2):
            s = s_bufs[slot][c]
            m_prev = m_ref[c]
            m_next = jnp.maximum(m_prev, jnp.max(s, axis=1, keepdims=True))
            al_bufs[slot][c] = jnp.exp2(m_prev - m_next)
            p_bufs[slot][c] = jnp.exp2(s - m_next).astype(BF16)
            m_ref[c] = m_next

    def weighted(j, slot):
        vc = vext_ref[pl.ds(pl.multiple_of(CTX + j * ATT_TK, CTX), ATT_TK), :]
        for c in range(2):
            pv = jnp.dot(p_bufs[slot][c], vc, preferred_element_type=F32)
            acc_ref[c] = acc_ref[c] * al_bufs[slot][c] + pv

    @pl.when(qi > 0)
    def _():
        scores(0, 0)
        scores(1, 1)
        softmax(0)

        def body(i, carry):
            j = 2 * i + 1
            scores(j + 1, 0)
            softmax(1)
            weighted(j - 1, 0)
            scores(j + 2, 1)
            softmax(0)
            weighted(j, 1)
            return carry

        lax.fori_loop(0, (n_lat - 2) // 2, body, 0)
        softmax(1)
        weighted(n_lat - 2, 0)
        weighted(n_lat - 1, 1)

    lv = lv_ref[...]
    lam = (jnp.exp(jnp.sum(lv[0:1] * lv[1:2], axis=1, keepdims=True))
           - jnp.exp(jnp.sum(lv[2:3] * lv[3:4], axis=1, keepdims=True)) + lam_init)
    a0 = acc_ref[0]
    a1 = acc_ref[1]
    o = a0[:, :LANES] / a0[:, LANES:] - lam * (a1[:, :LANES] / a1[:, LANES:])
    o_ref[...] = (_rms(o, sg_ref[...]) * (1.0 - lam_init)).astype(BF16)


def _attention(q, k, v, lam_vec, sub_g, lam_init):
    bsz, t, _ = q.shape
    nq = t // ATT_TQ
    kv = pl.BlockSpec((None, t, LANES), lambda b, h, i: (b, 0, h))
    qo = pl.BlockSpec((None, ATT_TQ, LANES), lambda b, h, i: (b, i, h))
    s_buf = pltpu.VMEM((2, ATT_TQ, ATT_TK), F32)
    p_buf = pltpu.VMEM((2, ATT_TQ, ATT_TK), BF16)
    al_buf = pltpu.VMEM((2, ATT_TQ, 1), F32)
    return pl.pallas_call(
        functools.partial(_attn_kernel, lam_init),
        grid=(bsz, DA_HEADS, nq),
        in_specs=[qo, kv, kv, _const_spec((4, DA_HEAD_DIM)), _const_spec((1, DA_V_DIM))],
        out_specs=qo,
        out_shape=jax.ShapeDtypeStruct((bsz, t, DA_WIDTH), BF16),
        scratch_shapes=[pltpu.VMEM((t, 2 * LANES), BF16),
                        pltpu.VMEM((2, ATT_TQ, 2 * LANES), F32),
                        pltpu.VMEM((2, ATT_TQ, 1), F32),
                        pltpu.VMEM((2, ATT_TQ, LANES), BF16),
                        s_buf, s_buf, p_buf, p_buf, al_buf, al_buf],
        compiler_params=_cparams(3),
        name="diff_attn",
    )(q, k, v, lam_vec, sub_g)


def _mixout_kernel(x_ref, at_ref, yf_ref, yb_ref, xc_ref, z_ref, dsk_ref, sg_ref, g_ref,
                   gate_ref, wa_ref, ws_ref, o_ref):
    tile = x_ref.shape[0]
    is_ctx = _row_is_ctx(pl.program_id(1), tile)
    gate = jnp.where(is_ctx, gate_ref[1:2, :], gate_ref[0:1, :])
    y = yf_ref[...] + yb_ref[...] + dsk_ref[...] * xc_ref[...].astype(F32)
    u = y * _silu(z_ref[...].astype(F32))
    ssd = _rms(u, sg_ref[...]).astype(BF16)
    mix = (jnp.dot(at_ref[...], wa_ref[...], preferred_element_type=F32)
           + jnp.dot(ssd, ws_ref[...], preferred_element_type=F32))
    o_ref[...] = x_ref[...] + gate * _rms(mix, g_ref[...])


def _mixout(xx, attn, yf, yb, xc, z, dsk, ssd_g, g, gate, wa, ws):
    bsz, t, _ = xx.shape
    nt = t // ROW_TILE
    row = lambda w: pl.BlockSpec((None, ROW_TILE, w), lambda b, i: (b, i, 0))
    mod = pl.BlockSpec((None, 2, D_MODEL), lambda b, i: (b, 0, 0))
    return pl.pallas_call(
        _mixout_kernel,
        grid=(bsz, nt),
        in_specs=[row(D_MODEL), row(DA_WIDTH), row(SSD_WIDTH), row(SSD_WIDTH),
                  row(SSD_WIDTH),
                  row(SSD_WIDTH), _const_spec((1, SSD_WIDTH)), _const_spec((1, SSD_WIDTH)),
                  _const_spec((1, D_MODEL)), mod, _const_spec(wa.shape), _const_spec(ws.shape)],
        out_specs=row(D_MODEL),
        out_shape=jax.ShapeDtypeStruct(xx.shape, F32),
        compiler_params=_cparams(2),
        name="mix_out",
    )(xx, attn, yf, yb, xc, z, dsk, ssd_g, g, gate, wa, ws)


def _ffn_kernel(x_ref, gin_ref, sh_ref, sc_ref, gout_ref, gate_ref, wg_ref, wu_ref, wo_ref,
                o_ref):
    tile = x_ref.shape[0]
    is_ctx = _row_is_ctx(pl.program_id(1), tile)
    pick = lambda r: jnp.where(is_ctx, r[1:2, :], r[0:1, :])
    x = x_ref[...]
    h = (_rms(x, gin_ref[...]) * (1.0 + pick(sc_ref)) + pick(sh_ref)).astype(BF16)
    acc = None
    for c in range(D_FF // FFN_CHUNK):
        sl = slice(c * FFN_CHUNK, (c + 1) * FFN_CHUNK)
        gt = jnp.dot(h, wg_ref[:, sl], preferred_element_type=F32)
        up = jnp.dot(h, wu_ref[:, sl], preferred_element_type=F32)
        a = (_silu(gt) * up).astype(BF16)
        part = jnp.dot(a, wo_ref[sl, :], preferred_element_type=F32)
        acc = part if acc is None else acc + part
    o_ref[...] = x + pick(gate_ref) * _rms(acc, gout_ref[...])


def _ffn(xx, gin, sh, sc, gout, gate, wg, wu, wo):
    bsz, t, _ = xx.shape
    nt = t // ROW_TILE
    row = pl.BlockSpec((None, ROW_TILE, D_MODEL), lambda b, i: (b, i, 0))
    mod = pl.BlockSpec((None, 2, D_MODEL), lambda b, i: (b, 0, 0))
    return pl.pallas_call(
        _ffn_kernel,
        grid=(bsz, nt),
        in_specs=[row, _const_spec((1, D_MODEL)), mod, mod, _const_spec((1, D_MODEL)), mod,
                  _const_spec(wg.shape), _const_spec(wu.shape), _const_spec(wo.shape)],
        out_specs=row,
        out_shape=jax.ShapeDtypeStruct(xx.shape, F32),
        compiler_params=_cparams(2),
        name="ffn",
    )(xx, gin, sh, sc, gout, gate, wg, wu, wo)


def _rope_tables(seq):
    rows = seq // GRID_W
    row = jnp.repeat(jnp.arange(rows, dtype=F32), GRID_W)
    col = jnp.tile(jnp.arange(GRID_W, dtype=F32), rows)
    nf = DA_HEAD_DIM // 4
    inv = ROPE_BASE ** (-jnp.arange(nf, dtype=F32) / nf)
    ang = jnp.concatenate([row[:, None] * inv, col[:, None] * inv], axis=-1)
    cos = jnp.concatenate([jnp.ones((CTX, DA_HEAD_DIM // 2), F32), jnp.cos(ang)], axis=0)
    sin = jnp.concatenate([jnp.zeros((CTX, DA_HEAD_DIM // 2), F32), jnp.sin(ang)], axis=0)
    return jnp.tile(cos, (1, 4)), jnp.tile(jnp.concatenate([-sin, sin], axis=1), (1, 2))


def kernel(x, c, ctx, c_ctx, w_ada, b_ada, norm_g, w_in, conv_w, conv_b, a_log, dt_bias,
           d_skip, ssd_norm_g, diff_lambda, subln_g, w_out, w_ffn_in, w_ffn_out):
    bsz, seq, _ = x.shape
    depth = w_ada.shape[0]
    t = CTX + seq
    assert ctx.shape[1] == CTX and t % ROW_TILE == 0 and seq % ATT_TK == 0

    xx = jnp.concatenate([ctx, x], axis=1)
    cond = jnp.zeros((8, D_MODEL), F32).at[:bsz].set(c).at[bsz].set(c_ctx)
    mod = _modulation(cond, w_ada, b_ada)
    cos_t, sin_t = _rope_tables(seq)

    q_end = DA_WIDTH
    k_end = 2 * DA_WIDTH
    v_end = 3 * DA_WIDTH
    z_end = v_end + SSD_WIDTH
    xbc_end = z_end + SSD_CONV_CH

    for i in range(depth):
        lam_init = 0.8 - 0.6 * math.exp(-0.3 * i)
        mods = [jnp.stack([mod[i, :bsz, j * D_MODEL:(j + 1) * D_MODEL],
                           jnp.broadcast_to(mod[i, bsz, j * D_MODEL:(j + 1) * D_MODEL],
                                            (bsz, D_MODEL))], axis=1) for j in range(N_MOD)]
        sh1, sc1, g1, sh2, sc2, g2 = mods
        wi = w_in[i].astype(BF16)
        wdt = jnp.zeros((D_MODEL, LANES), BF16).at[:, :2 * SSD_HEADS].set(wi[:, xbc_end:])
        lane16 = lambda a: jnp.zeros((1, LANES), F32).at[0, :2 * SSD_HEADS].set(a.reshape(-1))
        q, k, v, z, xbc, dt, dtt = _inproj(
            xx, norm_g[i, 0][None], sh1, sc1, cos_t, sin_t, lane16(dt_bias[i]),
            wi[:, :q_end], wi[:, q_end:k_end], wi[:, k_end:v_end], wi[:, v_end:z_end],
            wi[:, z_end:xbc_end], wdt)
        cw = jnp.zeros((8, SSD_CONV_CH), F32).at[:SSD_CONV].set(conv_w[i])
        xc = _conv(xbc, cw, conv_b[i][None])
        yf, yb = _ssd(xc, dt, dtt, lane16(a_log[i]), a_log[i].reshape(2 * SSD_HEADS, 1))
        attn = _attention(q, k, v, diff_lambda[i], subln_g[i][None], lam_init)
        wo = w_out[i].astype(BF16)
        xx = _mixout(xx, attn, yf, yb, xc, z, jnp.repeat(d_skip[i], SSD_HEAD_DIM)[None],
                     ssd_norm_g[i][None], norm_g[i, 1][None], g1, wo[:DA_WIDTH], wo[DA_WIDTH:])
        wf = w_ffn_in[i].astype(BF16)
        xx = _ffn(xx, norm_g[i, 2][None], sh2, sc2, norm_g[i, 3][None], g2,
                  wf[:, :D_FF], wf[:, D_FF:], w_ffn_out[i].astype(BF16))
    return xx[:, CTX:]
```

```python
import functools
import math

import jax
import jax.numpy as jnp
from jax import lax
from jax.experimental import pallas as pl
from jax.experimental.pallas import tpu as pltpu

F32 = jnp.float32
BF16 = jnp.bfloat16

D_MODEL = 1024
N_MOD = 6
CTX = 256
GRID_W = 64
DA_HEADS = 4
DA_HEAD_DIM = 64
DA_V_DIM = 2 * DA_HEAD_DIM
DA_WIDTH = DA_HEADS * DA_V_DIM
SSD_HEADS = 8
SSD_HEAD_DIM = 64
SSD_WIDTH = SSD_HEADS * SSD_HEAD_DIM
SSD_GROUPS = 2
SSD_STATE = 128
SSD_CONV = 5
SSD_CHUNK = 128
SSD_CONV_CH = SSD_WIDTH + 2 * SSD_GROUPS * SSD_STATE
D_FF = 2816
ROPE_BASE = 10000.0
EPS = 1e-6

LANES = 128
ROW_TILE = 640
CONV_TILE = 256
CONV_HALO = 16
ATT_TQ = 512
ATT_TK = 1280
ATT_UNROLL = 2
ATT_RB = 16
ATT_KT = 256
ATT_PV_ROWS = 256
FFN_CHUNK = 1408
VMEM_LIMIT = 56 * 1024 * 1024
LOG2E = 1.4426950408889634


def _cparams(n_axes, flags=None):
    return pltpu.CompilerParams(
        dimension_semantics=("arbitrary",) * n_axes, vmem_limit_bytes=VMEM_LIMIT, flags=flags)


def _rms(u, g):
    return u * lax.rsqrt(jnp.mean(u * u, axis=-1, keepdims=True) + EPS) * g


def _silu(u):
    return u * jax.nn.sigmoid(u)


def _row_is_ctx(step, tile):
    rows = step * tile + lax.broadcasted_iota(jnp.int32, (tile, 1), 0)
    return rows < CTX


def _const_spec(shape):
    zeros = (0,) * len(shape)
    return pl.BlockSpec(shape, lambda *_: zeros, pipeline_mode=pl.Buffered(1))


def _mod_kernel(c_ref, w_ref, b_ref, o_ref):
    s = _silu(c_ref[...])
    o_ref[...] = jnp.dot(s, w_ref[...], precision=lax.Precision.HIGHEST,
                         preferred_element_type=F32) + b_ref[...]


def _modulation(cond, w_ada, b_ada):
    depth = w_ada.shape[0]
    tn = 1536
    return pl.pallas_call(
        _mod_kernel,
        grid=(depth, N_MOD * D_MODEL // tn),
        in_specs=[pl.BlockSpec((8, D_MODEL), lambda i, j: (0, 0)),
                  pl.BlockSpec((None, D_MODEL, tn), lambda i, j: (i, 0, j)),
                  pl.BlockSpec((None, 1, tn), lambda i, j: (i, 0, j))],
        out_specs=pl.BlockSpec((None, 8, tn), lambda i, j: (i, 0, j)),
        out_shape=jax.ShapeDtypeStruct((depth, 8, N_MOD * D_MODEL), F32),
        compiler_params=_cparams(2),
        name="adaln_mod",
    )(cond, w_ada, b_ada.reshape(depth, 1, N_MOD * D_MODEL))


def _inproj_kernel(x_ref, g_ref, sh_ref, sc_ref, cos_ref, sin_ref, dtb_ref,
                   wq_ref, wk_ref, wv_ref, wz_ref, wx_ref, wdt_ref,
                   q_ref, k_ref, v_ref, z_ref, xbc_ref, dt_ref, dtt_ref):
    tile = x_ref.shape[0]
    is_ctx = _row_is_ctx(pl.program_id(1), tile)
    shift = jnp.where(is_ctx, sh_ref[1:2, :], sh_ref[0:1, :])
    scale = jnp.where(is_ctx, sc_ref[1:2, :], sc_ref[0:1, :])
    h = (_rms(x_ref[...], g_ref[...]) * (1.0 + scale) + shift).astype(BF16)

    cos = cos_ref[...]
    sin = sin_ref[...]
    lane = lax.broadcasted_iota(jnp.int32, (1, LANES), 1)
    first_half = (lane % DA_HEAD_DIM) < (DA_HEAD_DIM // 2)

    def rope(u):
        partner = jnp.where(first_half,
                            pltpu.roll(u, LANES - DA_HEAD_DIM // 2, axis=1),
                            pltpu.roll(u, DA_HEAD_DIM // 2, axis=1))
        return u * cos + partner * sin

    q = jnp.dot(h, wq_ref[...], preferred_element_type=F32)
    k = jnp.dot(h, wk_ref[...], preferred_element_type=F32)
    q_scale = DA_HEAD_DIM ** -0.5 * LOG2E
    for hd in range(DA_HEADS):
        sl = slice(hd * LANES, (hd + 1) * LANES)
        q_ref[:, sl] = (rope(q[:, sl]) * q_scale).astype(BF16)
        k_ref[:, sl] = rope(k[:, sl]).astype(BF16)
    v_ref[...] = jnp.dot(h, wv_ref[...], preferred_element_type=F32).astype(BF16)
    z_ref[...] = jnp.dot(h, wz_ref[...], preferred_element_type=F32).astype(BF16)
    xbc_ref[...] = jnp.dot(h, wx_ref[...], preferred_element_type=F32).astype(BF16)
    dt_raw = jnp.dot(h, wdt_ref[...], preferred_element_type=F32) + dtb_ref[...]
    dt = jnp.maximum(dt_raw, 0.0) + jnp.log1p(jnp.exp(-jnp.abs(dt_raw)))
    dt_ref[...] = dt
    dtt_ref[...] = dt.T[:2 * SSD_HEADS, :]


def _inproj(xx, g, sh, sc, cos_t, sin_t, dtb, wq, wk, wv, wz, wx, wdt):
    bsz, t, _ = xx.shape
    nt = t // ROW_TILE
    row = lambda w: pl.BlockSpec((None, ROW_TILE, w), lambda b, i: (b, i, 0))
    mod = pl.BlockSpec((None, 2, D_MODEL), lambda b, i: (b, 0, 0))
    tab = pl.BlockSpec((ROW_TILE, LANES), lambda b, i: (i, 0))
    return pl.pallas_call(
        _inproj_kernel,
        grid=(bsz, nt),
        in_specs=[row(D_MODEL), _const_spec((1, D_MODEL)), mod, mod, tab, tab,
                  _const_spec((1, LANES)),
                  _const_spec(wq.shape), _const_spec(wk.shape), _const_spec(wv.shape),
                  _const_spec(wz.shape), _const_spec(wx.shape), _const_spec(wdt.shape)],
        out_specs=[row(DA_WIDTH), row(DA_WIDTH), row(DA_WIDTH), row(SSD_WIDTH),
                   row(SSD_CONV_CH), row(LANES),
                   pl.BlockSpec((None, 2 * SSD_HEADS, ROW_TILE), lambda b, i: (b, 0, i))],
        out_shape=[jax.ShapeDtypeStruct((bsz, t, DA_WIDTH), BF16),
                   jax.ShapeDtypeStruct((bsz, t, DA_WIDTH), BF16),
                   jax.ShapeDtypeStruct((bsz, t, DA_WIDTH), BF16),
                   jax.ShapeDtypeStruct((bsz, t, SSD_WIDTH), BF16),
                   jax.ShapeDtypeStruct((bsz, t, SSD_CONV_CH), BF16),
                   jax.ShapeDtypeStruct((bsz, t, LANES), F32),
                   jax.ShapeDtypeStruct((bsz, 2 * SSD_HEADS, t), F32)],
        compiler_params=_cparams(2),
        name="inproj",
    )(xx, g, sh, sc, cos_t, sin_t, dtb, wq, wk, wv, wz, wx, wdt)


def _conv_kernel(cur_ref, prev_ref, next_ref, w_ref, b_ref, o_ref, ext_ref):
    i = pl.program_id(1)
    n = pl.num_programs(1)
    half = SSD_CONV // 2
    pad = 8
    first_of_segment = jnp.logical_or(i == 0, i == CTX // CONV_TILE)
    last_of_segment = jnp.logical_or(i == CTX // CONV_TILE - 1, i == n - 1)
    prev = prev_ref[...].astype(F32)[CONV_HALO - pad:, :]
    nxt = next_ref[...].astype(F32)[:pad, :]
    ext_ref[0:pad, :] = jnp.where(first_of_segment, 0.0, prev)
    ext_ref[pad:pad + CONV_TILE, :] = cur_ref[...].astype(F32)
    ext_ref[pad + CONV_TILE:, :] = jnp.where(last_of_segment, 0.0, nxt)
    acc = b_ref[...] + w_ref[half:half + 1, :] * ext_ref[pad:pad + CONV_TILE, :]
    for s in range(-half, half + 1):
        if s != 0:
            acc = acc + w_ref[half + s:half + s + 1, :] * ext_ref[pl.ds(pad + s, CONV_TILE), :]
    o_ref[...] = _silu(acc).astype(BF16)


def _conv(xbc, w, b):
    bsz, t, ch = xbc.shape
    nt = t // CONV_TILE
    per = CONV_TILE // CONV_HALO
    last_halo = t // CONV_HALO - 1
    return pl.pallas_call(
        _conv_kernel,
        grid=(bsz, nt),
        in_specs=[pl.BlockSpec((None, CONV_TILE, ch), lambda b_, i: (b_, i, 0)),
                  pl.BlockSpec((None, CONV_HALO, ch),
                               lambda b_, i: (b_, jnp.maximum(i * per - 1, 0), 0)),
                  pl.BlockSpec((None, CONV_HALO, ch),
                               lambda b_, i: (b_, jnp.minimum((i + 1) * per, last_halo), 0)),
                  _const_spec((8, ch)), _const_spec((1, ch))],
        out_specs=pl.BlockSpec((None, CONV_TILE, ch), lambda b_, i: (b_, i, 0)),
        out_shape=jax.ShapeDtypeStruct((bsz, t, ch), BF16),
        scratch_shapes=[pltpu.VMEM((CONV_TILE + 16, ch), F32)],
        compiler_params=_cparams(2),
        name="dwconv",
    )(xbc, xbc, xbc, w, b)


def _ssd_direction(d, xc_ref, dt_ref, dtt_ref, arow_ref, acol_ref, s_ref, y_ref):
    q = SSD_CHUNK
    xall = xc_ref[...]
    dt = dt_ref[...]
    dtt = dtt_ref[...]
    da = dt * (-jnp.exp(arow_ref[...]))
    dat = dtt * (-jnp.exp(acol_ref[...]))
    ii = lax.broadcasted_iota(jnp.int32, (q, q), 0)
    jj = lax.broadcasted_iota(jnp.int32, (q, q), 1)
    mask = (jj <= ii) if d == 0 else (jj >= ii)
    mf = mask.astype(F32)
    hi = lax.Precision.HIGHEST
    cum_col = jnp.dot(mf, da, precision=hi, preferred_element_type=F32)
    cum_row = lax.dot_general(dat, mf, (((1,), (1,)), ((), ())), precision=hi,
                              preferred_element_type=F32)
    tot = jnp.sum(dat, axis=1, keepdims=True)
    lane = lax.broadcasted_iota(jnp.int32, (1, LANES), 1)
    low_half = lane < SSD_HEAD_DIM
    heads_per_group = SSD_HEADS // SSD_GROUPS
    for g in range(SSD_GROUPS):
        bg = xall[:, SSD_WIDTH + g * SSD_STATE:SSD_WIDTH + (g + 1) * SSD_STATE]
        c0 = SSD_WIDTH + SSD_GROUPS * SSD_STATE + g * SSD_STATE
        cg = xall[:, c0:c0 + SSD_STATE]
        cb = lax.dot_general(cg, bg, (((1,), (1,)), ((), ())), preferred_element_type=F32)
        bgt = bg.astype(F32).T
        cgf = cg.astype(F32)
        for pr in range(heads_per_group // 2):
            pair = g * (heads_per_group // 2) + pr
            xp = xall[:, pair * LANES:(pair + 1) * LANES]
            sp = s_ref[pair]
            spb = sp.astype(BF16)
            ys, us, decs = [], [], []
            for r in range(2):
                ln = 8 * d + 2 * pair + r
                colb = cum_col[:, ln:ln + 1]
                rowb = cum_row[ln:ln + 1, :]
                dtr = dtt[ln:ln + 1, :]
                tot_h = tot[ln:ln + 1, :]
                lmat = jnp.exp(jnp.where(mask, colb - rowb, -jnp.inf))
                w = (cb * lmat * dtr).astype(BF16)
                e = (cgf * jnp.exp(colb)).astype(BF16)
                ys.append(jnp.dot(w, xp, preferred_element_type=F32)
                          + jnp.dot(e, spb, preferred_element_type=F32))
                wrow = dtr * jnp.exp(tot_h - rowb)
                us.append(jnp.dot((bgt * wrow).astype(BF16), xp, preferred_element_type=F32))
                decs.append(jnp.exp(tot_h))
            y_ref[:, pair * LANES:(pair + 1) * LANES] = jnp.where(low_half, ys[0], ys[1])
            dec = jnp.where(low_half, decs[0], decs[1])
            s_ref[pair] = sp * dec + jnp.where(low_half, us[0], us[1])


def _ssd_kernel(xf_ref, dtf_ref, dttf_ref, xb_ref, dtb_ref, dttb_ref, arow_ref, acol_ref,
                yf_ref, yb_ref, s_ref):
    @pl.when(pl.program_id(1) == 0)
    def _():
        s_ref[...] = jnp.zeros_like(s_ref)

    _ssd_direction(0, xf_ref, dtf_ref, dttf_ref, arow_ref, acol_ref, s_ref.at[0], yf_ref)
    _ssd_direction(1, xb_ref, dtb_ref, dttb_ref, arow_ref, acol_ref, s_ref.at[1], yb_ref)


def _ssd(xc, dt, dtt, arow, acol):
    bsz, t, ch = xc.shape
    nc = t // SSD_CHUNK
    nctx = CTX // SSD_CHUNK

    def fwd(c):
        return c

    def bwd(c):
        return jnp.where(c < nctx, nctx - 1 - c, nc - 1 + nctx - c)

    def specs(order):
        return [pl.BlockSpec((None, SSD_CHUNK, ch), lambda b, c: (b, order(c), 0)),
                pl.BlockSpec((None, SSD_CHUNK, LANES), lambda b, c: (b, order(c), 0)),
                pl.BlockSpec((None, 2 * SSD_HEADS, SSD_CHUNK), lambda b, c: (b, 0, order(c)))]

    y_spec = lambda order: pl.BlockSpec((None, SSD_CHUNK, SSD_WIDTH),
                                        lambda b, c: (b, order(c), 0))
    y_shape = jax.ShapeDtypeStruct((bsz, t, SSD_WIDTH), F32)
    return pl.pallas_call(
        _ssd_kernel,
        grid=(bsz, nc),
        in_specs=specs(fwd) + specs(bwd) + [_const_spec((1, LANES)),
                                            _const_spec((2 * SSD_HEADS, 1))],
        out_specs=[y_spec(fwd), y_spec(bwd)],
        out_shape=[y_shape, y_shape],
        scratch_shapes=[pltpu.VMEM((2, SSD_HEADS // 2, SSD_STATE, LANES), F32)],
        compiler_params=_cparams(2),
        name="ssd_scan",
    )(xc, dt, dtt, xc, dt, dtt, arow, acol)


_NT = (((1,), (1,)), ((), ()))


def _masked_q(q):
    lane = lax.broadcasted_iota(jnp.int32, (1, LANES), 1)
    comp0 = lane < DA_HEAD_DIM
    zero = jnp.zeros_like(q)
    return jnp.where(comp0, q, zero), jnp.where(comp0, zero, q)


def _diff_out(lam_init, acc0, acc1, lv_ref, sg_ref):
    lv = lv_ref[...]
    lam = (jnp.exp(jnp.sum(lv[0:1] * lv[1:2], axis=1, keepdims=True))
           - jnp.exp(jnp.sum(lv[2:3] * lv[3:4], axis=1, keepdims=True)) + lam_init)
    o = acc0[:, :LANES] / acc0[:, LANES:] - lam * (acc1[:, :LANES] / acc1[:, LANES:])
    return (_rms(o, sg_ref[...]) * (1.0 - lam_init)).astype(BF16)


def _attn_ctx_kernel(lam_init, q_ref, k_ref, v_ref, lv_ref, sg_ref, o_ref):
    kc = k_ref[...]
    vext = jnp.concatenate([v_ref[...], jnp.ones((CTX, LANES), BF16)], axis=1)
    accs = []
    for qm in _masked_q(q_ref[...]):
        s = lax.dot_general(qm, kc, _NT, preferred_element_type=F32)
        p = jnp.exp2(s - jnp.max(s, axis=1, keepdims=True)).astype(BF16)
        accs.append(jnp.dot(p, vext, preferred_element_type=F32))
    o_ref[...] = _diff_out(lam_init, accs[0], accs[1], lv_ref, sg_ref)


def _attn_lat_kernel(lam_init, qa_ref, qb_ref, k_ref, v_ref, lv_ref, sg_ref, o_ref,
                     vext_ref, acc_ref, m_ref, qm_ref, s0_ref, s1_ref, p0_ref, p1_ref,
                     al0_ref, al1_ref):
    t = k_ref.shape[0]
    n = t // ATT_TK
    assert t % ATT_TK == 0 and n >= 3
    s_bufs, p_bufs, al_bufs = (s0_ref, s1_ref), (p0_ref, p1_ref), (al0_ref, al1_ref)

    @pl.when(pl.program_id(2) == 0)
    def _():
        vext_ref[:, :LANES] = v_ref[...]
        vext_ref[:, LANES:] = jnp.ones((t, LANES), BF16)

    half = ATT_TQ // 2
    for c, (qa, qb) in enumerate(zip(_masked_q(qa_ref[...]), _masked_q(qb_ref[...]))):
        qm_ref[c * ATT_TQ:c * ATT_TQ + half, :] = qa
        qm_ref[c * ATT_TQ + half:(c + 1) * ATT_TQ, :] = qb
    acc_ref[...] = jnp.zeros_like(acc_ref)

    key_tiles = ATT_TK // ATT_KT
    rows_all = 2 * ATT_TQ

    def aligned(start):
        return start if isinstance(start, int) else pl.multiple_of(start, LANES)

    def scores_piece(j, slot, kt):
        kc = k_ref[pl.ds(aligned(j * ATT_TK + kt * ATT_KT), ATT_KT), :]
        s_bufs[slot][:, kt * ATT_KT:(kt + 1) * ATT_KT] = lax.dot_general(
            qm_ref[...], kc, _NT, preferred_element_type=F32)

    def softmax_piece(slot, r, first):
        rows = slice(r, r + ATT_RB)
        s = s_bufs[slot][rows, :]
        m_cur = jnp.max(s, axis=1, keepdims=True)
        if first:
            m_next = jnp.broadcast_to(m_cur, (ATT_RB, LANES))
            al_bufs[slot][rows, :] = jnp.ones((ATT_RB, LANES), F32)
        else:
            m_prev = m_ref[rows, :]
            m_next = jnp.maximum(m_prev, m_cur)
            al_bufs[slot][rows, :] = jnp.exp2(m_prev - m_next)
        m_ref[rows, :] = m_next
        m_wide = jnp.concatenate([m_next] * (ATT_TK // LANES), axis=1)
        p_bufs[slot][rows, :] = jnp.exp2(s - m_wide).astype(BF16)

    def weighted_piece(j, slot, seg):
        rows = slice(seg * ATT_PV_ROWS, (seg + 1) * ATT_PV_ROWS)
        vc = vext_ref[pl.ds(aligned(j * ATT_TK), ATT_TK), :]
        pv = jnp.dot(p_bufs[slot][rows, :], vc, preferred_element_type=F32)
        al = al_bufs[slot][rows, :]
        acc_ref[rows, :] = acc_ref[rows, :] * jnp.concatenate([al, al], axis=1) + pv

    def trip(j, par, scores=True, softmax=True, weighted=True, first=False):
        nseg = rows_all // ATT_PV_ROWS
        blocks = rows_all // ATT_RB
        for seg in range(nseg):
            if weighted:
                weighted_piece(j - 1, 1 - par, seg)
            if scores:
                for kt in range(seg * key_tiles // nseg, (seg + 1) * key_tiles // nseg):
                    scores_piece(j + 1, 1 - par, kt)
            if softmax:
                for blk in range(seg * blocks // nseg, (seg + 1) * blocks // nseg):
                    softmax_piece(par, blk * ATT_RB, first)

    trip(-1, 1, softmax=False, weighted=False)
    trip(0, 0, weighted=False, first=True)
    peel = (n - 2) % ATT_UNROLL
    for j in range(1, 1 + peel):
        trip(j, j % 2)

    def body(i, carry):
        for u in range(ATT_UNROLL):
            trip(ATT_UNROLL * i + 1 + peel + u, (1 + peel + u) % 2)
        return carry

    lax.fori_loop(0, (n - 2) // ATT_UNROLL, body, 0)
    trip(n - 1, (n - 1) % 2, scores=False)
    trip(n, n % 2, scores=False, softmax=False)
    o_ref[...] = _diff_out(lam_init, acc_ref[:ATT_TQ, :], acc_ref[ATT_TQ:, :], lv_ref, sg_ref)


def _attention(q, k, v, lam_vec, sub_g, lam_init):
    bsz, t, _ = q.shape
    seq = t - CTX
    part = functools.partial
    consts = [_const_spec((4, DA_HEAD_DIM)), _const_spec((1, DA_V_DIM))]
    ctx_blk = pl.BlockSpec((None, CTX, LANES), lambda b, h: (b, 0, h))
    o_ctx = pl.pallas_call(
        part(_attn_ctx_kernel, lam_init),
        grid=(bsz, DA_HEADS),
        in_specs=[ctx_blk, ctx_blk, ctx_blk] + consts,
        out_specs=ctx_blk,
        out_shape=jax.ShapeDtypeStruct((bsz, CTX, DA_WIDTH), BF16),
        compiler_params=_cparams(2),
        name="diff_attn_ctx",
    )(q, k, v, lam_vec, sub_g)

    assert ATT_TQ == 2 * CTX
    kv = pl.BlockSpec((None, t, LANES), lambda b, h, i: (b, 0, h),
                      pipeline_mode=pl.Buffered(1))
    qa = pl.BlockSpec((None, CTX, LANES), lambda b, h, i: (b, 2 * i + 1, h))
    qb = pl.BlockSpec((None, CTX, LANES), lambda b, h, i: (b, 2 * i + 2, h))
    s_buf = pltpu.VMEM((2 * ATT_TQ, ATT_TK), F32)
    p_buf = pltpu.VMEM((2 * ATT_TQ, ATT_TK), BF16)
    al_buf = pltpu.VMEM((2 * ATT_TQ, LANES), F32)
    o_lat = pl.pallas_call(
        part(_attn_lat_kernel, lam_init),
        grid=(bsz, DA_HEADS, seq // ATT_TQ),
        in_specs=[qa, qb, kv, kv] + consts,
        out_specs=pl.BlockSpec((None, ATT_TQ, LANES), lambda b, h, i: (b, i, h)),
        out_shape=jax.ShapeDtypeStruct((bsz, seq, DA_WIDTH), BF16),
        scratch_shapes=[pltpu.VMEM((t, 2 * LANES), BF16),
                        pltpu.VMEM((2 * ATT_TQ, 2 * LANES), F32),
                        pltpu.VMEM((2 * ATT_TQ, LANES), F32),
                        pltpu.VMEM((2 * ATT_TQ, LANES), BF16),
                        s_buf, s_buf, p_buf, p_buf, al_buf, al_buf],
        compiler_params=_cparams(3),
        name="diff_attn",
    )(q, q, k, v, lam_vec, sub_g)
    return o_ctx, o_lat


def _mixout_kernel(x_ref, actx_ref, alat_ref, yf_ref, yb_ref, xc_ref, z_ref, dsk_ref, sg_ref,
                   g_ref, gate_ref, wa_ref, ws_ref, o_ref):
    is_ctx = pl.program_id(1) == 0
    gate = jnp.where(is_ctx, gate_ref[1:2, :], gate_ref[0:1, :])
    attn = jnp.where(is_ctx, actx_ref[...], alat_ref[...])
    y = yf_ref[...] + yb_ref[...] + dsk_ref[...] * xc_ref[...].astype(F32)
    u = y * _silu(z_ref[...].astype(F32))
    ssd = _rms(u, sg_ref[...]).astype(BF16)
    mix = (jnp.dot(attn, wa_ref[...], preferred_element_type=F32)
           + jnp.dot(ssd, ws_ref[...], preferred_element_type=F32))
    o_ref[...] = x_ref[...] + gate * _rms(mix, g_ref[...])


def _mixout(xx, attn_ctx, attn_lat, yf, yb, xc, z, dsk, ssd_g, g, gate, wa, ws):
    bsz, t, _ = xx.shape
    nt = t // CTX
    row = lambda w: pl.BlockSpec((None, CTX, w), lambda b, i: (b, i, 0))
    mod = pl.BlockSpec((None, 2, D_MODEL), lambda b, i: (b, 0, 0))
    return pl.pallas_call(
        _mixout_kernel,
        grid=(bsz, nt),
        in_specs=[row(D_MODEL),
                  pl.BlockSpec((None, CTX, DA_WIDTH), lambda b, i: (b, 0, 0)),
                  pl.BlockSpec((None, CTX, DA_WIDTH), lambda b, i: (b, jnp.maximum(i - 1, 0), 0)),
                  row(SSD_WIDTH), row(SSD_WIDTH),
                  row(SSD_WIDTH),
                  row(SSD_WIDTH), _const_spec((1, SSD_WIDTH)), _const_spec((1, SSD_WIDTH)),
                  _const_spec((1, D_MODEL)), mod, _const_spec(wa.shape), _const_spec(ws.shape)],
        out_specs=row(D_MODEL),
        out_shape=jax.ShapeDtypeStruct(xx.shape, F32),
        compiler_params=_cparams(2),
        name="mix_out",
    )(xx, attn_ctx, attn_lat, yf, yb, xc, z, dsk, ssd_g, g, gate, wa, ws)


def _ffn_kernel(x_ref, gin_ref, sh_ref, sc_ref, gout_ref, gate_ref, wg_ref, wu_ref, wo_ref,
                o_ref):
    tile = x_ref.shape[0]
    is_ctx = _row_is_ctx(pl.program_id(1), tile)
    pick = lambda r: jnp.where(is_ctx, r[1:2, :], r[0:1, :])
    x = x_ref[...]
    h = (_rms(x, gin_ref[...]) * (1.0 + pick(sc_ref)) + pick(sh_ref)).astype(BF16)
    acc = None
    for c in range(D_FF // FFN_CHUNK):
        sl = slice(c * FFN_CHUNK, (c + 1) * FFN_CHUNK)
        gt = jnp.dot(h, wg_ref[:, sl], preferred_element_type=F32)
        up = jnp.dot(h, wu_ref[:, sl], preferred_element_type=F32)
        a = (_silu(gt) * up).astype(BF16)
        part = jnp.dot(a, wo_ref[sl, :], preferred_element_type=F32)
        acc = part if acc is None else acc + part
    o_ref[...] = x + pick(gate_ref) * _rms(acc, gout_ref[...])


def _ffn(xx, gin, sh, sc, gout, gate, wg, wu, wo):
    bsz, t, _ = xx.shape
    nt = t // ROW_TILE
    row = pl.BlockSpec((None, ROW_TILE, D_MODEL), lambda b, i: (b, i, 0))
    mod = pl.BlockSpec((None, 2, D_MODEL), lambda b, i: (b, 0, 0))
    return pl.pallas_call(
        _ffn_kernel,
        grid=(bsz, nt),
        in_specs=[row, _const_spec((1, D_MODEL)), mod, mod, _const_spec((1, D_MODEL)), mod,
                  _const_spec(wg.shape), _const_spec(wu.shape), _const_spec(wo.shape)],
        out_specs=row,
        out_shape=jax.ShapeDtypeStruct(xx.shape, F32),
        compiler_params=_cparams(2),
        name="ffn",
    )(xx, gin, sh, sc, gout, gate, wg, wu, wo)


def _rope_tables(seq):
    rows = seq // GRID_W
    row = jnp.repeat(jnp.arange(rows, dtype=F32), GRID_W)
    col = jnp.tile(jnp.arange(GRID_W, dtype=F32), rows)
    nf = DA_HEAD_DIM // 4
    inv = ROPE_BASE ** (-jnp.arange(nf, dtype=F32) / nf)
    ang = jnp.concatenate([row[:, None] * inv, col[:, None] * inv], axis=-1)
    cos = jnp.concatenate([jnp.ones((CTX, DA_HEAD_DIM // 2), F32), jnp.cos(ang)], axis=0)
    sin = jnp.concatenate([jnp.zeros((CTX, DA_HEAD_DIM // 2), F32), jnp.sin(ang)], axis=0)
    return jnp.tile(cos, (1, 4)), jnp.tile(jnp.concatenate([-sin, sin], axis=1), (1, 2))


def kernel(x, c, ctx, c_ctx, w_ada, b_ada, norm_g, w_in, conv_w, conv_b, a_log, dt_bias,
           d_skip, ssd_norm_g, diff_lambda, subln_g, w_out, w_ffn_in, w_ffn_out):
    bsz, seq, _ = x.shape
    depth = w_ada.shape[0]
    t = CTX + seq
    assert ctx.shape[1] == CTX and t % ROW_TILE == 0 and seq % ATT_TQ == 0

    xx = jnp.concatenate([ctx, x], axis=1)
    cond = jnp.zeros((8, D_MODEL), F32).at[:bsz].set(c).at[bsz].set(c_ctx)
    mod = _modulation(cond, w_ada, b_ada)
    cos_t, sin_t = _rope_tables(seq)

    q_end = DA_WIDTH
    k_end = 2 * DA_WIDTH
    v_end = 3 * DA_WIDTH
    z_end = v_end + SSD_WIDTH
    xbc_end = z_end + SSD_CONV_CH

    for i in range(depth):
        lam_init = 0.8 - 0.6 * math.exp(-0.3 * i)
        mods = [jnp.stack([mod[i, :bsz, j * D_MODEL:(j + 1) * D_MODEL],
                           jnp.broadcast_to(mod[i, bsz, j * D_MODEL:(j + 1) * D_MODEL],
                                            (bsz, D_MODEL))], axis=1) for j in range(N_MOD)]
        sh1, sc1, g1, sh2, sc2, g2 = mods
        wi = w_in[i].astype(BF16)
        wdt = jnp.zeros((D_MODEL, LANES), BF16).at[:, :2 * SSD_HEADS].set(wi[:, xbc_end:])
        lane16 = lambda a: jnp.zeros((1, LANES), F32).at[0, :2 * SSD_HEADS].set(a.reshape(-1))
        q, k, v, z, xbc, dt, dtt = _inproj(
            xx, norm_g[i, 0][None], sh1, sc1, cos_t, sin_t, lane16(dt_bias[i]),
            wi[:, :q_end], wi[:, q_end:k_end], wi[:, k_end:v_end], wi[:, v_end:z_end],
            wi[:, z_end:xbc_end], wdt)
        cw = jnp.zeros((8, SSD_CONV_CH), F32).at[:SSD_CONV].set(conv_w[i])
        xc = _conv(xbc, cw, conv_b[i][None])
        yf, yb = _ssd(xc, dt, dtt, lane16(a_log[i]), a_log[i].reshape(2 * SSD_HEADS, 1))
        attn_ctx, attn_lat = _attention(q, k, v, diff_lambda[i], subln_g[i][None], lam_init)
        wo = w_out[i].astype(BF16)
        xx = _mixout(xx, attn_ctx, attn_lat, yf, yb, xc, z,
                     jnp.repeat(d_skip[i], SSD_HEAD_DIM)[None], ssd_norm_g[i][None],
                     norm_g[i, 1][None], g1, wo[:DA_WIDTH], wo[DA_WIDTH:])
        wf = w_ffn_in[i].astype(BF16)
        xx = _ffn(xx, norm_g[i, 2][None], sh2, sc2, norm_g[i, 3][None], g2,
                  wf[:, :D_FF], wf[:, D_FF:], w_ffn_out[i].astype(BF16))
    return xx[:, CTX:]
```

```python
import functools
import math

import jax
import jax.numpy as jnp
from jax import lax
from jax.experimental import pallas as pl
from jax.experimental.pallas import tpu as pltpu

F32 = jnp.float32
BF16 = jnp.bfloat16

D_MODEL = 1024
N_MOD = 6
CTX = 256
GRID_W = 64
DA_HEADS = 4
DA_HEAD_DIM = 64
DA_V_DIM = 2 * DA_HEAD_DIM
DA_WIDTH = DA_HEADS * DA_V_DIM
SSD_HEADS = 8
SSD_HEAD_DIM = 64
SSD_WIDTH = SSD_HEADS * SSD_HEAD_DIM
SSD_GROUPS = 2
SSD_STATE = 128
SSD_CONV = 5
SSD_CHUNK = 128
SSD_CONV_CH = SSD_WIDTH + 2 * SSD_GROUPS * SSD_STATE
D_FF = 2816
ROPE_BASE = 10000.0
EPS = 1e-6

LANES = 128
ROW_TILE = 640
CONV_TILE = 256
CONV_HALO = 16
ATT_TQ = 512
ATT_TK = 1280
ATT_SLOTS = 3
ATT_RB = 16
ATT_KT = 256
ATT_PV_ROWS = 256
FFN_CHUNK = 1408
VMEM_LIMIT = 56 * 1024 * 1024
LOG2E = 1.4426950408889634
NEG_BIG = -1e30


def _cparams(n_axes, flags=None):
    return pltpu.CompilerParams(
        dimension_semantics=("arbitrary",) * n_axes, vmem_limit_bytes=VMEM_LIMIT, flags=flags)


def _rms(u, g):
    return u * lax.rsqrt(jnp.mean(u * u, axis=-1, keepdims=True) + EPS) * g


def _silu(u):
    return u * jax.nn.sigmoid(u)


def _row_is_ctx(step, tile):
    rows = step * tile + lax.broadcasted_iota(jnp.int32, (tile, 1), 0)
    return rows < CTX


def _const_spec(shape):
    zeros = (0,) * len(shape)
    return pl.BlockSpec(shape, lambda *_: zeros, pipeline_mode=pl.Buffered(1))


def _mod_kernel(c_ref, w_ref, b_ref, o_ref):
    s = _silu(c_ref[...])
    o_ref[...] = jnp.dot(s, w_ref[...], precision=lax.Precision.HIGHEST,
                         preferred_element_type=F32) + b_ref[...]


def _modulation(cond, w_ada, b_ada):
    depth = w_ada.shape[0]
    tn = 1536
    return pl.pallas_call(
        _mod_kernel,
        grid=(depth, N_MOD * D_MODEL // tn),
        in_specs=[pl.BlockSpec((8, D_MODEL), lambda i, j: (0, 0)),
                  pl.BlockSpec((None, D_MODEL, tn), lambda i, j: (i, 0, j)),
                  pl.BlockSpec((None, 1, tn), lambda i, j: (i, 0, j))],
        out_specs=pl.BlockSpec((None, 8, tn), lambda i, j: (i, 0, j)),
        out_shape=jax.ShapeDtypeStruct((depth, 8, N_MOD * D_MODEL), F32),
        compiler_params=_cparams(2),
        name="adaln_mod",
    )(cond, w_ada, b_ada.reshape(depth, 1, N_MOD * D_MODEL))


def _inproj_kernel(x_ref, g_ref, sh_ref, sc_ref, cos_ref, sin_ref, dtb_ref,
                   wq_ref, wk_ref, wv_ref, wz_ref, wx_ref, wdt_ref,
                   q_ref, k_ref, v_ref, z_ref, xbc_ref, dt_ref, dtt_ref):
    tile = x_ref.shape[0]
    is_ctx = _row_is_ctx(pl.program_id(1), tile)
    shift = jnp.where(is_ctx, sh_ref[1:2, :], sh_ref[0:1, :])
    scale = jnp.where(is_ctx, sc_ref[1:2, :], sc_ref[0:1, :])
    h = (_rms(x_ref[...], g_ref[...]) * (1.0 + scale) + shift).astype(BF16)

    cos = cos_ref[...]
    sin = sin_ref[...]
    lane = lax.broadcasted_iota(jnp.int32, (1, LANES), 1)
    first_half = (lane % DA_HEAD_DIM) < (DA_HEAD_DIM // 2)

    def rope(u):
        partner = jnp.where(first_half,
                            pltpu.roll(u, LANES - DA_HEAD_DIM // 2, axis=1),
                            pltpu.roll(u, DA_HEAD_DIM // 2, axis=1))
        return u * cos + partner * sin

    q = jnp.dot(h, wq_ref[...], preferred_element_type=F32)
    k = jnp.dot(h, wk_ref[...], preferred_element_type=F32)
    q_scale = DA_HEAD_DIM ** -0.5 * LOG2E
    for hd in range(DA_HEADS):
        sl = slice(hd * LANES, (hd + 1) * LANES)
        q_ref[:, sl] = (rope(q[:, sl]) * q_scale).astype(BF16)
        k_ref[:, sl] = rope(k[:, sl]).astype(BF16)
    v_ref[...] = jnp.dot(h, wv_ref[...], preferred_element_type=F32).astype(BF16)
    z_ref[...] = jnp.dot(h, wz_ref[...], preferred_element_type=F32).astype(BF16)
    xbc_ref[...] = jnp.dot(h, wx_ref[...], preferred_element_type=F32).astype(BF16)
    dt_raw = jnp.dot(h, wdt_ref[...], preferred_element_type=F32) + dtb_ref[...]
    dt = jnp.maximum(dt_raw, 0.0) + jnp.log1p(jnp.exp(-jnp.abs(dt_raw)))
    dt_ref[...] = dt
    dtt_ref[...] = dt.T[:2 * SSD_HEADS, :]


def _inproj(xx, g, sh, sc, cos_t, sin_t, dtb, wq, wk, wv, wz, wx, wdt):
    bsz, t, _ = xx.shape
    nt = t // ROW_TILE
    row = lambda w: pl.BlockSpec((None, ROW_TILE, w), lambda b, i: (b, i, 0))
    mod = pl.BlockSpec((None, 2, D_MODEL), lambda b, i: (b, 0, 0))
    tab = pl.BlockSpec((ROW_TILE, LANES), lambda b, i: (i, 0))
    return pl.pallas_call(
        _inproj_kernel,
        grid=(bsz, nt),
        in_specs=[row(D_MODEL), _const_spec((1, D_MODEL)), mod, mod, tab, tab,
                  _const_spec((1, LANES)),
                  _const_spec(wq.shape), _const_spec(wk.shape), _const_spec(wv.shape),
                  _const_spec(wz.shape), _const_spec(wx.shape), _const_spec(wdt.shape)],
        out_specs=[row(DA_WIDTH), row(DA_WIDTH), row(DA_WIDTH), row(SSD_WIDTH),
                   row(SSD_CONV_CH), row(LANES),
                   pl.BlockSpec((None, 2 * SSD_HEADS, ROW_TILE), lambda b, i: (b, 0, i))],
        out_shape=[jax.ShapeDtypeStruct((bsz, t, DA_WIDTH), BF16),
                   jax.ShapeDtypeStruct((bsz, t, DA_WIDTH), BF16),
                   jax.ShapeDtypeStruct((bsz, t, DA_WIDTH), BF16),
                   jax.ShapeDtypeStruct((bsz, t, SSD_WIDTH), BF16),
                   jax.ShapeDtypeStruct((bsz, t, SSD_CONV_CH), BF16),
                   jax.ShapeDtypeStruct((bsz, t, LANES), F32),
                   jax.ShapeDtypeStruct((bsz, 2 * SSD_HEADS, t), F32)],
        compiler_params=_cparams(2),
        name="inproj",
    )(xx, g, sh, sc, cos_t, sin_t, dtb, wq, wk, wv, wz, wx, wdt)


def _conv_kernel(cur_ref, prev_ref, next_ref, w_ref, b_ref, o_ref, ext_ref):
    i = pl.program_id(1)
    n = pl.num_programs(1)
    half = SSD_CONV // 2
    pad = 8
    first_of_segment = jnp.logical_or(i == 0, i == CTX // CONV_TILE)
    last_of_segment = jnp.logical_or(i == CTX // CONV_TILE - 1, i == n - 1)
    prev = prev_ref[...].astype(F32)[CONV_HALO - pad:, :]
    nxt = next_ref[...].astype(F32)[:pad, :]
    ext_ref[0:pad, :] = jnp.where(first_of_segment, 0.0, prev)
    ext_ref[pad:pad + CONV_TILE, :] = cur_ref[...].astype(F32)
    ext_ref[pad + CONV_TILE:, :] = jnp.where(last_of_segment, 0.0, nxt)
    acc = b_ref[...] + w_ref[half:half + 1, :] * ext_ref[pad:pad + CONV_TILE, :]
    for s in range(-half, half + 1):
        if s != 0:
            acc = acc + w_ref[half + s:half + s + 1, :] * ext_ref[pl.ds(pad + s, CONV_TILE), :]
    o_ref[...] = _silu(acc).astype(BF16)


def _conv(xbc, w, b):
    bsz, t, ch = xbc.shape
    nt = t // CONV_TILE
    per = CONV_TILE // CONV_HALO
    last_halo = t // CONV_HALO - 1
    return pl.pallas_call(
        _conv_kernel,
        grid=(bsz, nt),
        in_specs=[pl.BlockSpec((None, CONV_TILE, ch), lambda b_, i: (b_, i, 0)),
                  pl.BlockSpec((None, CONV_HALO, ch),
                               lambda b_, i: (b_, jnp.maximum(i * per - 1, 0), 0)),
                  pl.BlockSpec((None, CONV_HALO, ch),
                               lambda b_, i: (b_, jnp.minimum((i + 1) * per, last_halo), 0)),
                  _const_spec((8, ch)), _const_spec((1, ch))],
        out_specs=pl.BlockSpec((None, CONV_TILE, ch), lambda b_, i: (b_, i, 0)),
        out_shape=jax.ShapeDtypeStruct((bsz, t, ch), BF16),
        scratch_shapes=[pltpu.VMEM((CONV_TILE + 16, ch), F32)],
        compiler_params=_cparams(2),
        name="dwconv",
    )(xbc, xbc, xbc, w, b)


def _ssd_direction(d, xc_ref, dt_ref, dtt_ref, arow_ref, acol_ref, s_ref, y_ref):
    q = SSD_CHUNK
    xall = xc_ref[...]
    dt = dt_ref[...]
    dtt = dtt_ref[...]
    da = dt * (-jnp.exp(arow_ref[...]))
    dat = dtt * (-jnp.exp(acol_ref[...]))
    ii = lax.broadcasted_iota(jnp.int32, (q, q), 0)
    jj = lax.broadcasted_iota(jnp.int32, (q, q), 1)
    mask = (jj <= ii) if d == 0 else (jj >= ii)
    mf = mask.astype(F32)
    hi = lax.Precision.HIGHEST
    cum_col = jnp.dot(mf, da, precision=hi, preferred_element_type=F32)
    cum_row = lax.dot_general(dat, mf, (((1,), (1,)), ((), ())), precision=hi,
                              preferred_element_type=F32)
    tot = jnp.sum(dat, axis=1, keepdims=True)
    lane = lax.broadcasted_iota(jnp.int32, (1, LANES), 1)
    low_half = lane < SSD_HEAD_DIM
    heads_per_group = SSD_HEADS // SSD_GROUPS
    for g in range(SSD_GROUPS):
        bg = xall[:, SSD_WIDTH + g * SSD_STATE:SSD_WIDTH + (g + 1) * SSD_STATE]
        c0 = SSD_WIDTH + SSD_GROUPS * SSD_STATE + g * SSD_STATE
        cg = xall[:, c0:c0 + SSD_STATE]
        cb = lax.dot_general(cg, bg, (((1,), (1,)), ((), ())), preferred_element_type=F32)
        bgt = bg.astype(F32).T
        cgf = cg.astype(F32)
        for pr in range(heads_per_group // 2):
            pair = g * (heads_per_group // 2) + pr
            xp = xall[:, pair * LANES:(pair + 1) * LANES]
            sp = s_ref[pair]
            spb = sp.astype(BF16)
            ys, us, decs = [], [], []
            for r in range(2):
                ln = 8 * d + 2 * pair + r
                colb = cum_col[:, ln:ln + 1]
                rowb = cum_row[ln:ln + 1, :]
                dtr = dtt[ln:ln + 1, :]
                tot_h = tot[ln:ln + 1, :]
                lmat = jnp.exp(jnp.where(mask, colb - rowb, -jnp.inf))
                w = (cb * lmat * dtr).astype(BF16)
                e = (cgf * jnp.exp(colb)).astype(BF16)
                ys.append(jnp.dot(w, xp, preferred_element_type=F32)
                          + jnp.dot(e, spb, preferred_element_type=F32))
                wrow = dtr * jnp.exp(tot_h - rowb)
                us.append(jnp.dot((bgt * wrow).astype(BF16), xp, preferred_element_type=F32))
                decs.append(jnp.exp(tot_h))
            y_ref[:, pair * LANES:(pair + 1) * LANES] = jnp.where(low_half, ys[0], ys[1])
            dec = jnp.where(low_half, decs[0], decs[1])
            s_ref[pair] = sp * dec + jnp.where(low_half, us[0], us[1])


def _ssd_kernel(xf_ref, dtf_ref, dttf_ref, xb_ref, dtb_ref, dttb_ref, arow_ref, acol_ref,
                yf_ref, yb_ref, s_ref):
    @pl.when(pl.program_id(1) == 0)
    def _():
        s_ref[...] = jnp.zeros_like(s_ref)

    _ssd_direction(0, xf_ref, dtf_ref, dttf_ref, arow_ref, acol_ref, s_ref.at[0], yf_ref)
    _ssd_direction(1, xb_ref, dtb_ref, dttb_ref, arow_ref, acol_ref, s_ref.at[1], yb_ref)


def _ssd(xc, dt, dtt, arow, acol):
    bsz, t, ch = xc.shape
    nc = t // SSD_CHUNK
    nctx = CTX // SSD_CHUNK

    def fwd(c):
        return c

    def bwd(c):
        return jnp.where(c < nctx, nctx - 1 - c, nc - 1 + nctx - c)

    def specs(order):
        return [pl.BlockSpec((None, SSD_CHUNK, ch), lambda b, c: (b, order(c), 0)),
                pl.BlockSpec((None, SSD_CHUNK, LANES), lambda b, c: (b, order(c), 0)),
                pl.BlockSpec((None, 2 * SSD_HEADS, SSD_CHUNK), lambda b, c: (b, 0, order(c)))]

    y_spec = lambda order: pl.BlockSpec((None, SSD_CHUNK, SSD_WIDTH),
                                        lambda b, c: (b, order(c), 0))
    y_shape = jax.ShapeDtypeStruct((bsz, t, SSD_WIDTH), F32)
    return pl.pallas_call(
        _ssd_kernel,
        grid=(bsz, nc),
        in_specs=specs(fwd) + specs(bwd) + [_const_spec((1, LANES)),
                                            _const_spec((2 * SSD_HEADS, 1))],
        out_specs=[y_spec(fwd), y_spec(bwd)],
        out_shape=[y_shape, y_shape],
        scratch_shapes=[pltpu.VMEM((2, SSD_HEADS // 2, SSD_STATE, LANES), F32)],
        compiler_params=_cparams(2),
        name="ssd_scan",
    )(xc, dt, dtt, xc, dt, dtt, arow, acol)


_NT = (((1,), (1,)), ((), ()))


def _masked_q(q):
    lane = lax.broadcasted_iota(jnp.int32, (1, LANES), 1)
    comp0 = lane < DA_HEAD_DIM
    zero = jnp.zeros_like(q)
    return jnp.where(comp0, q, zero), jnp.where(comp0, zero, q)


def _diff_out(lam_init, acc0, acc1, lv_ref, sg_ref):
    lv = lv_ref[...]
    lam = (jnp.exp(jnp.sum(lv[0:1] * lv[1:2], axis=1, keepdims=True))
           - jnp.exp(jnp.sum(lv[2:3] * lv[3:4], axis=1, keepdims=True)) + lam_init)
    o = acc0[:, :LANES] / acc0[:, LANES:] - lam * (acc1[:, :LANES] / acc1[:, LANES:])
    return (_rms(o, sg_ref[...]) * (1.0 - lam_init)).astype(BF16)


def _attn_ctx_kernel(lam_init, q_ref, k_ref, v_ref, lv_ref, sg_ref, o_ref):
    kc = k_ref[...]
    vext = jnp.concatenate([v_ref[...], jnp.ones((CTX, LANES), BF16)], axis=1)
    accs = []
    for qm in _masked_q(q_ref[...]):
        s = lax.dot_general(qm, kc, _NT, preferred_element_type=F32)
        p = jnp.exp2(s - jnp.max(s, axis=1, keepdims=True)).astype(BF16)
        accs.append(jnp.dot(p, vext, preferred_element_type=F32))
    o_ref[...] = _diff_out(lam_init, accs[0], accs[1], lv_ref, sg_ref)


def _attn_lat_kernel(lam_init, qa_ref, qb_ref, k_ref, v_ref, lv_ref, sg_ref, o_ref,
                     vext_ref, acc_ref, m_ref, qm_ref, p_ref, al_ref, *s_bufs):
    t = k_ref.shape[0]
    n = t // ATT_TK
    assert t % ATT_TK == 0 and n >= 2 and len(s_bufs) == ATT_SLOTS

    @pl.when(pl.program_id(2) == 0)
    def _():
        vext_ref[:, :LANES] = v_ref[...]
        vext_ref[:, LANES:] = jnp.ones((t, LANES), BF16)

    half = ATT_TQ // 2
    for c, (qa, qb) in enumerate(zip(_masked_q(qa_ref[...]), _masked_q(qb_ref[...]))):
        qm_ref[c * ATT_TQ:c * ATT_TQ + half, :] = qa
        qm_ref[c * ATT_TQ + half:(c + 1) * ATT_TQ, :] = qb
    acc_ref[...] = jnp.zeros_like(acc_ref)
    m_ref[...] = jnp.full(m_ref.shape, NEG_BIG, F32)

    key_tiles = ATT_TK // ATT_KT
    rows_all = 2 * ATT_TQ
    nseg = rows_all // ATT_PV_ROWS

    def aligned(start):
        return start if isinstance(start, int) else pl.multiple_of(start, LANES)

    def scores_piece(j, slot, kt):
        kc = k_ref[pl.ds(aligned(j * ATT_TK + kt * ATT_KT), ATT_KT), :]
        s_bufs[slot][:, kt * ATT_KT:(kt + 1) * ATT_KT] = lax.dot_general(
            qm_ref[...], kc, _NT, preferred_element_type=F32)

    def softmax_piece(slot, r):
        rows = slice(r, r + ATT_RB)
        s = s_bufs[slot][rows, :]
        m_prev = m_ref[rows, :]
        m_next = jnp.maximum(m_prev, jnp.max(s, axis=1, keepdims=True))
        al_ref[rows, :] = jnp.exp2(m_prev - m_next)
        m_ref[rows, :] = m_next
        m_wide = jnp.concatenate([m_next] * (ATT_TK // LANES), axis=1)
        p_ref[rows, :] = jnp.exp2(s - m_wide).astype(BF16)

    def weighted_piece(j, seg):
        rows = slice(seg * ATT_PV_ROWS, (seg + 1) * ATT_PV_ROWS)
        vc = vext_ref[pl.ds(aligned(j * ATT_TK), ATT_TK), :]
        pv = jnp.dot(p_ref[rows, :], vc, preferred_element_type=F32)
        al = al_ref[rows, :]
        acc_ref[rows, :] = acc_ref[rows, :] * jnp.concatenate([al, al], axis=1) + pv

    def trip(j, slot, scores=True):
        for seg in range(nseg):
            for r in range(seg * ATT_PV_ROWS, (seg + 1) * ATT_PV_ROWS, ATT_RB):
                softmax_piece(slot, r)
            if scores:
                for kt in range(seg * key_tiles // nseg, (seg + 1) * key_tiles // nseg):
                    scores_piece(j + 1, (slot + 1) % ATT_SLOTS, kt)
            weighted_piece(j, seg)

    for kt in range(key_tiles):
        scores_piece(0, 0, kt)
    peel = (n - 1) % ATT_SLOTS
    for j in range(peel):
        trip(j, j % ATT_SLOTS)

    def body(i, carry):
        for u in range(ATT_SLOTS):
            trip(ATT_SLOTS * i + peel + u, (peel + u) % ATT_SLOTS)
        return carry

    lax.fori_loop(0, (n - 1) // ATT_SLOTS, body, 0)
    trip(n - 1, (n - 1) % ATT_SLOTS, scores=False)
    o_ref[...] = _diff_out(lam_init, acc_ref[:ATT_TQ, :], acc_ref[ATT_TQ:, :], lv_ref, sg_ref)


def _attention(q, k, v, lam_vec, sub_g, lam_init):
    bsz, t, _ = q.shape
    seq = t - CTX
    part = functools.partial
    consts = [_const_spec((4, DA_HEAD_DIM)), _const_spec((1, DA_V_DIM))]
    ctx_blk = pl.BlockSpec((None, CTX, LANES), lambda b, h: (b, 0, h))
    o_ctx = pl.pallas_call(
        part(_attn_ctx_kernel, lam_init),
        grid=(bsz, DA_HEADS),
        in_specs=[ctx_blk, ctx_blk, ctx_blk] + consts,
        out_specs=ctx_blk,
        out_shape=jax.ShapeDtypeStruct((bsz, CTX, DA_WIDTH), BF16),
        compiler_params=_cparams(2),
        name="diff_attn_ctx",
    )(q, k, v, lam_vec, sub_g)

    assert ATT_TQ == 2 * CTX
    kv = pl.BlockSpec((None, t, LANES), lambda b, h, i: (b, 0, h),
                      pipeline_mode=pl.Buffered(1))
    qa = pl.BlockSpec((None, CTX, LANES), lambda b, h, i: (b, 2 * i + 1, h))
    qb = pl.BlockSpec((None, CTX, LANES), lambda b, h, i: (b, 2 * i + 2, h))
    s_buf = pltpu.VMEM((2 * ATT_TQ, ATT_TK), F32)
    p_buf = pltpu.VMEM((2 * ATT_TQ, ATT_TK), BF16)
    al_buf = pltpu.VMEM((2 * ATT_TQ, LANES), F32)
    o_lat = pl.pallas_call(
        part(_attn_lat_kernel, lam_init),
        grid=(bsz, DA_HEADS, seq // ATT_TQ),
        in_specs=[qa, qb, kv, kv] + consts,
        out_specs=pl.BlockSpec((None, ATT_TQ, LANES), lambda b, h, i: (b, i, h)),
        out_shape=jax.ShapeDtypeStruct((bsz, seq, DA_WIDTH), BF16),
        scratch_shapes=[pltpu.VMEM((t, 2 * LANES), BF16),
                        pltpu.VMEM((2 * ATT_TQ, 2 * LANES), F32),
                        pltpu.VMEM((2 * ATT_TQ, LANES), F32),
                        pltpu.VMEM((2 * ATT_TQ, LANES), BF16), p_buf, al_buf]
                       + [s_buf] * ATT_SLOTS,
        compiler_params=_cparams(3),
        name="diff_attn",
    )(q, q, k, v, lam_vec, sub_g)
    return o_ctx, o_lat


def _mixout_kernel(x_ref, actx_ref, alat_ref, yf_ref, yb_ref, xc_ref, z_ref, dsk_ref, sg_ref,
                   g_ref, gate_ref, wa_ref, ws_ref, o_ref):
    is_ctx = pl.program_id(1) == 0
    gate = jnp.where(is_ctx, gate_ref[1:2, :], gate_ref[0:1, :])
    attn = jnp.where(is_ctx, actx_ref[...], alat_ref[...])
    y = yf_ref[...] + yb_ref[...] + dsk_ref[...] * xc_ref[...].astype(F32)
    u = y * _silu(z_ref[...].astype(F32))
    ssd = _rms(u, sg_ref[...]).astype(BF16)
    mix = (jnp.dot(attn, wa_ref[...], preferred_element_type=F32)
           + jnp.dot(ssd, ws_ref[...], preferred_element_type=F32))
    o_ref[...] = x_ref[...] + gate * _rms(mix, g_ref[...])


def _mixout(xx, attn_ctx, attn_lat, yf, yb, xc, z, dsk, ssd_g, g, gate, wa, ws):
    bsz, t, _ = xx.shape
    nt = t // CTX
    row = lambda w: pl.BlockSpec((None, CTX, w), lambda b, i: (b, i, 0))
    mod = pl.BlockSpec((None, 2, D_MODEL), lambda b, i: (b, 0, 0))
    return pl.pallas_call(
        _mixout_kernel,
        grid=(bsz, nt),
        in_specs=[row(D_MODEL),
                  pl.BlockSpec((None, CTX, DA_WIDTH), lambda b, i: (b, 0, 0)),
                  pl.BlockSpec((None, CTX, DA_WIDTH), lambda b, i: (b, jnp.maximum(i - 1, 0), 0)),
                  row(SSD_WIDTH), row(SSD_WIDTH),
                  row(SSD_WIDTH),
                  row(SSD_WIDTH), _const_spec((1, SSD_WIDTH)), _const_spec((1, SSD_WIDTH)),
                  _const_spec((1, D_MODEL)), mod, _const_spec(wa.shape), _const_spec(ws.shape)],
        out_specs=row(D_MODEL),
        out_shape=jax.ShapeDtypeStruct(xx.shape, F32),
        compiler_params=_cparams(2),
        name="mix_out",
    )(xx, attn_ctx, attn_lat, yf, yb, xc, z, dsk, ssd_g, g, gate, wa, ws)


def _ffn_kernel(x_ref, gin_ref, sh_ref, sc_ref, gout_ref, gate_ref, wg_ref, wu_ref, wo_ref,
                o_ref):
    tile = x_ref.shape[0]
    is_ctx = _row_is_ctx(pl.program_id(1), tile)
    pick = lambda r: jnp.where(is_ctx, r[1:2, :], r[0:1, :])
    x = x_ref[...]
    h = (_rms(x, gin_ref[...]) * (1.0 + pick(sc_ref)) + pick(sh_ref)).astype(BF16)
    acc = None
    for c in range(D_FF // FFN_CHUNK):
        sl = slice(c * FFN_CHUNK, (c + 1) * FFN_CHUNK)
        gt = jnp.dot(h, wg_ref[:, sl], preferred_element_type=F32)
        up = jnp.dot(h, wu_ref[:, sl], preferred_element_type=F32)
        a = (_silu(gt) * up).astype(BF16)
        part = jnp.dot(a, wo_ref[sl, :], preferred_element_type=F32)
        acc = part if acc is None else acc + part
    o_ref[...] = x + pick(gate_ref) * _rms(acc, gout_ref[...])


def _ffn(xx, gin, sh, sc, gout, gate, wg, wu, wo):
    bsz, t, _ = xx.shape
    nt = t // ROW_TILE
    row = pl.BlockSpec((None, ROW_TILE, D_MODEL), lambda b, i: (b, i, 0))
    mod = pl.BlockSpec((None, 2, D_MODEL), lambda b, i: (b, 0, 0))
    return pl.pallas_call(
        _ffn_kernel,
        grid=(bsz, nt),
        in_specs=[row, _const_spec((1, D_MODEL)), mod, mod, _const_spec((1, D_MODEL)), mod,
                  _const_spec(wg.shape), _const_spec(wu.shape), _const_spec(wo.shape)],
        out_specs=row,
        out_shape=jax.ShapeDtypeStruct(xx.shape, F32),
        compiler_params=_cparams(2),
        name="ffn",
    )(xx, gin, sh, sc, gout, gate, wg, wu, wo)


def _rope_tables(seq):
    rows = seq // GRID_W
    row = jnp.repeat(jnp.arange(rows, dtype=F32), GRID_W)
    col = jnp.tile(jnp.arange(GRID_W, dtype=F32), rows)
    nf = DA_HEAD_DIM // 4
    inv = ROPE_BASE ** (-jnp.arange(nf, dtype=F32) / nf)
    ang = jnp.concatenate([row[:, None] * inv, col[:, None] * inv], axis=-1)
    cos = jnp.concatenate([jnp.ones((CTX, DA_HEAD_DIM // 2), F32), jnp.cos(ang)], axis=0)
    sin = jnp.concatenate([jnp.zeros((CTX, DA_HEAD_DIM // 2), F32), jnp.sin(ang)], axis=0)
    return jnp.tile(cos, (1, 4)), jnp.tile(jnp.concatenate([-sin, sin], axis=1), (1, 2))


def kernel(x, c, ctx, c_ctx, w_ada, b_ada, norm_g, w_in, conv_w, conv_b, a_log, dt_bias,
           d_skip, ssd_norm_g, diff_lambda, subln_g, w_out, w_ffn_in, w_ffn_out):
    bsz, seq, _ = x.shape
    depth = w_ada.shape[0]
    t = CTX + seq
    assert ctx.shape[1] == CTX and t % ROW_TILE == 0 and seq % ATT_TQ == 0

    xx = jnp.concatenate([ctx, x], axis=1)
    cond = jnp.zeros((8, D_MODEL), F32).at[:bsz].set(c).at[bsz].set(c_ctx)
    mod = _modulation(cond, w_ada, b_ada)
    cos_t, sin_t = _rope_tables(seq)

    q_end = DA_WIDTH
    k_end = 2 * DA_WIDTH
    v_end = 3 * DA_WIDTH
    z_end = v_end + SSD_WIDTH
    xbc_end = z_end + SSD_CONV_CH

    for i in range(depth):
        lam_init = 0.8 - 0.6 * math.exp(-0.3 * i)
        mods = [jnp.stack([mod[i, :bsz, j * D_MODEL:(j + 1) * D_MODEL],
                           jnp.broadcast_to(mod[i, bsz, j * D_MODEL:(j + 1) * D_MODEL],
                                            (bsz, D_MODEL))], axis=1) for j in range(N_MOD)]
        sh1, sc1, g1, sh2, sc2, g2 = mods
        wi = w_in[i].astype(BF16)
        wdt = jnp.zeros((D_MODEL, LANES), BF16).at[:, :2 * SSD_HEADS].set(wi[:, xbc_end:])
        lane16 = lambda a: jnp.zeros((1, LANES), F32).at[0, :2 * SSD_HEADS].set(a.reshape(-1))
        q, k, v, z, xbc, dt, dtt = _inproj(
            xx, norm_g[i, 0][None], sh1, sc1, cos_t, sin_t, lane16(dt_bias[i]),
            wi[:, :q_end], wi[:, q_end:k_end], wi[:, k_end:v_end], wi[:, v_end:z_end],
            wi[:, z_end:xbc_end], wdt)
        cw = jnp.zeros((8, SSD_CONV_CH), F32).at[:SSD_CONV].set(conv_w[i])
        xc = _conv(xbc, cw, conv_b[i][None])
        yf, yb = _ssd(xc, dt, dtt, lane16(a_log[i]), a_log[i].reshape(2 * SSD_HEADS, 1))
        attn_ctx, attn_lat = _attention(q, k, v, diff_lambda[i], subln_g[i][None], lam_init)
        wo = w_out[i].astype(BF16)
        xx = _mixout(xx, attn_ctx, attn_lat, yf, yb, xc, z,
                     jnp.repeat(d_skip[i], SSD_HEAD_DIM)[None], ssd_norm_g[i][None],
                     norm_g[i, 1][None], g1, wo[:DA_WIDTH], wo[DA_WIDTH:])
        wf = w_ffn_in[i].astype(BF16)
        xx = _ffn(xx, norm_g[i, 2][None], sh2, sc2, norm_g[i, 3][None], g2,
                  wf[:, :D_FF], wf[:, D_FF:], w_ffn_out[i].astype(BF16))
    return xx[:, CTX:]
```

```python
import functools
import math

import jax
import jax.numpy as jnp
from jax import lax
from jax.experimental import pallas as pl
from jax.experimental.pallas import tpu as pltpu

F32 = jnp.float32
BF16 = jnp.bfloat16

D_MODEL = 1024
N_MOD = 6
CTX = 256
GRID_W = 64
DA_HEADS = 4
DA_HEAD_DIM = 64
DA_V_DIM = 2 * DA_HEAD_DIM
DA_WIDTH = DA_HEADS * DA_V_DIM
SSD_HEADS = 8
SSD_HEAD_DIM = 64
SSD_WIDTH = SSD_HEADS * SSD_HEAD_DIM
SSD_GROUPS = 2
SSD_STATE = 128
SSD_CONV = 5
SSD_CHUNK = 128
SSD_CONV_CH = SSD_WIDTH + 2 * SSD_GROUPS * SSD_STATE
D_FF = 2816
ROPE_BASE = 10000.0
EPS = 1e-6

LANES = 128
ROW_TILE = 640
CONV_TILE = 256
CONV_HALO = 16
ATT_TQ = 512
ATT_TK = 1280
ATT_SLOTS = 3
ATT_UNROLL = 12
ATT_RB = 16
ATT_KT = 256
ATT_PV_ROWS = 256
FFN_CHUNK = 1408
FFN_ROWS = 320
MIX_BLOCKS = 5
VMEM_LIMIT = 56 * 1024 * 1024
LOG2E = 1.4426950408889634
NEG_BIG = -1e30


def _cparams(n_axes, flags=None):
    return pltpu.CompilerParams(
        dimension_semantics=("arbitrary",) * n_axes, vmem_limit_bytes=VMEM_LIMIT, flags=flags)


def _rms(u, g):
    return u * lax.rsqrt(jnp.mean(u * u, axis=-1, keepdims=True) + EPS) * g


def _silu(u):
    return u * jax.nn.sigmoid(u)


def _row_is_ctx(step, tile):
    rows = step * tile + lax.broadcasted_iota(jnp.int32, (tile, 1), 0)
    return rows < CTX


def _const_spec(shape):
    zeros = (0,) * len(shape)
    return pl.BlockSpec(shape, lambda *_: zeros, pipeline_mode=pl.Buffered(1))


def _mod_kernel(c_ref, w_ref, b_ref, o_ref):
    s = _silu(c_ref[...])
    o_ref[...] = jnp.dot(s, w_ref[...], precision=lax.Precision.HIGHEST,
                         preferred_element_type=F32) + b_ref[...]


def _modulation(cond, w_ada, b_ada):
    depth = w_ada.shape[0]
    tn = 1536
    return pl.pallas_call(
        _mod_kernel,
        grid=(depth, N_MOD * D_MODEL // tn),
        in_specs=[pl.BlockSpec((8, D_MODEL), lambda i, j: (0, 0)),
                  pl.BlockSpec((None, D_MODEL, tn), lambda i, j: (i, 0, j)),
                  pl.BlockSpec((None, 1, tn), lambda i, j: (i, 0, j))],
        out_specs=pl.BlockSpec((None, 8, tn), lambda i, j: (i, 0, j)),
        out_shape=jax.ShapeDtypeStruct((depth, 8, N_MOD * D_MODEL), F32),
        compiler_params=_cparams(2),
        name="adaln_mod",
    )(cond, w_ada, b_ada.reshape(depth, 1, N_MOD * D_MODEL))


def _inproj_kernel(x_ref, g_ref, sh_ref, sc_ref, cos_ref, sin_ref, dtb_ref,
                   wq_ref, wk_ref, wv_ref, wz_ref, wx_ref, wdt_ref,
                   q_ref, k_ref, v_ref, z_ref, xbc_ref, dt_ref, dtt_ref):
    tile = x_ref.shape[0]
    is_ctx = _row_is_ctx(pl.program_id(1), tile)
    shift = jnp.where(is_ctx, sh_ref[1:2, :], sh_ref[0:1, :])
    scale = jnp.where(is_ctx, sc_ref[1:2, :], sc_ref[0:1, :])
    h = (_rms(x_ref[...], g_ref[...]) * (1.0 + scale) + shift).astype(BF16)

    cos = cos_ref[...]
    sin = sin_ref[...]
    lane = lax.broadcasted_iota(jnp.int32, (1, LANES), 1)
    first_half = (lane % DA_HEAD_DIM) < (DA_HEAD_DIM // 2)

    def rope(u):
        partner = jnp.where(first_half,
                            pltpu.roll(u, LANES - DA_HEAD_DIM // 2, axis=1),
                            pltpu.roll(u, DA_HEAD_DIM // 2, axis=1))
        return u * cos + partner * sin

    q = jnp.dot(h, wq_ref[...], preferred_element_type=F32)
    k = jnp.dot(h, wk_ref[...], preferred_element_type=F32)
    q_scale = DA_HEAD_DIM ** -0.5 * LOG2E
    for hd in range(DA_HEADS):
        sl = slice(hd * LANES, (hd + 1) * LANES)
        q_ref[:, sl] = (rope(q[:, sl]) * q_scale).astype(BF16)
        k_ref[:, sl] = rope(k[:, sl]).astype(BF16)
    v_ref[...] = jnp.dot(h, wv_ref[...], preferred_element_type=F32).astype(BF16)
    z_ref[...] = jnp.dot(h, wz_ref[...], preferred_element_type=F32).astype(BF16)
    xbc_ref[...] = jnp.dot(h, wx_ref[...], preferred_element_type=F32).astype(BF16)
    dt_raw = jnp.dot(h, wdt_ref[...], preferred_element_type=F32) + dtb_ref[...]
    dt = jnp.maximum(dt_raw, 0.0) + jnp.log1p(jnp.exp(-jnp.abs(dt_raw)))
    dt_ref[...] = dt
    dtt_ref[...] = dt.T[:2 * SSD_HEADS, :]


def _inproj(xx, g, sh, sc, cos_t, sin_t, dtb, wq, wk, wv, wz, wx, wdt):
    bsz, t, _ = xx.shape
    nt = t // ROW_TILE
    row = lambda w: pl.BlockSpec((None, ROW_TILE, w), lambda b, i: (b, i, 0))
    mod = pl.BlockSpec((None, 2, D_MODEL), lambda b, i: (b, 0, 0))
    tab = pl.BlockSpec((ROW_TILE, LANES), lambda b, i: (i, 0))
    return pl.pallas_call(
        _inproj_kernel,
        grid=(bsz, nt),
        in_specs=[row(D_MODEL), _const_spec((1, D_MODEL)), mod, mod, tab, tab,
                  _const_spec((1, LANES)),
                  _const_spec(wq.shape), _const_spec(wk.shape), _const_spec(wv.shape),
                  _const_spec(wz.shape), _const_spec(wx.shape), _const_spec(wdt.shape)],
        out_specs=[row(DA_WIDTH), row(DA_WIDTH), row(DA_WIDTH), row(SSD_WIDTH),
                   row(SSD_CONV_CH), row(LANES),
                   pl.BlockSpec((None, 2 * SSD_HEADS, ROW_TILE), lambda b, i: (b, 0, i))],
        out_shape=[jax.ShapeDtypeStruct((bsz, t, DA_WIDTH), BF16),
                   jax.ShapeDtypeStruct((bsz, t, DA_WIDTH), BF16),
                   jax.ShapeDtypeStruct((bsz, t, DA_WIDTH), BF16),
                   jax.ShapeDtypeStruct((bsz, t, SSD_WIDTH), BF16),
                   jax.ShapeDtypeStruct((bsz, t, SSD_CONV_CH), BF16),
                   jax.ShapeDtypeStruct((bsz, t, LANES), F32),
                   jax.ShapeDtypeStruct((bsz, 2 * SSD_HEADS, t), F32)],
        compiler_params=_cparams(2),
        name="inproj",
    )(xx, g, sh, sc, cos_t, sin_t, dtb, wq, wk, wv, wz, wx, wdt)


def _conv_kernel(cur_ref, prev_ref, next_ref, w_ref, b_ref, o_ref, ext_ref):
    i = pl.program_id(1)
    n = pl.num_programs(1)
    half = SSD_CONV // 2
    pad = 8
    first_of_segment = jnp.logical_or(i == 0, i == CTX // CONV_TILE)
    last_of_segment = jnp.logical_or(i == CTX // CONV_TILE - 1, i == n - 1)
    prev = prev_ref[...].astype(F32)[CONV_HALO - pad:, :]
    nxt = next_ref[...].astype(F32)[:pad, :]
    ext_ref[0:pad, :] = jnp.where(first_of_segment, 0.0, prev)
    ext_ref[pad:pad + CONV_TILE, :] = cur_ref[...].astype(F32)
    ext_ref[pad + CONV_TILE:, :] = jnp.where(last_of_segment, 0.0, nxt)
    acc = b_ref[...] + w_ref[half:half + 1, :] * ext_ref[pad:pad + CONV_TILE, :]
    for s in range(-half, half + 1):
        if s != 0:
            acc = acc + w_ref[half + s:half + s + 1, :] * ext_ref[pl.ds(pad + s, CONV_TILE), :]
    o_ref[...] = _silu(acc).astype(BF16)


def _conv(xbc, w, b):
    bsz, t, ch = xbc.shape
    nt = t // CONV_TILE
    per = CONV_TILE // CONV_HALO
    last_halo = t // CONV_HALO - 1
    return pl.pallas_call(
        _conv_kernel,
        grid=(bsz, nt),
        in_specs=[pl.BlockSpec((None, CONV_TILE, ch), lambda b_, i: (b_, i, 0)),
                  pl.BlockSpec((None, CONV_HALO, ch),
                               lambda b_, i: (b_, jnp.maximum(i * per - 1, 0), 0)),
                  pl.BlockSpec((None, CONV_HALO, ch),
                               lambda b_, i: (b_, jnp.minimum((i + 1) * per, last_halo), 0)),
                  _const_spec((8, ch)), _const_spec((1, ch))],
        out_specs=pl.BlockSpec((None, CONV_TILE, ch), lambda b_, i: (b_, i, 0)),
        out_shape=jax.ShapeDtypeStruct((bsz, t, ch), BF16),
        scratch_shapes=[pltpu.VMEM((CONV_TILE + 16, ch), F32)],
        compiler_params=_cparams(2),
        name="dwconv",
    )(xbc, xbc, xbc, w, b)


def _ssd_direction(d, xc_ref, dt_ref, dtt_ref, arow_ref, acol_ref, s_ref, y_ref):
    q = SSD_CHUNK
    xall = xc_ref[...]
    dt = dt_ref[...]
    dtt = dtt_ref[...]
    da = dt * (-jnp.exp(arow_ref[...]))
    dat = dtt * (-jnp.exp(acol_ref[...]))
    ii = lax.broadcasted_iota(jnp.int32, (q, q), 0)
    jj = lax.broadcasted_iota(jnp.int32, (q, q), 1)
    mask = (jj <= ii) if d == 0 else (jj >= ii)
    mf = mask.astype(F32)
    hi = lax.Precision.HIGHEST
    cum_col = jnp.dot(mf, da, precision=hi, preferred_element_type=F32)
    cum_row = lax.dot_general(dat, mf, (((1,), (1,)), ((), ())), precision=hi,
                              preferred_element_type=F32)
    tot = jnp.sum(dat, axis=1, keepdims=True)
    lane = lax.broadcasted_iota(jnp.int32, (1, LANES), 1)
    low_half = lane < SSD_HEAD_DIM
    heads_per_group = SSD_HEADS // SSD_GROUPS
    for g in range(SSD_GROUPS):
        bg = xall[:, SSD_WIDTH + g * SSD_STATE:SSD_WIDTH + (g + 1) * SSD_STATE]
        c0 = SSD_WIDTH + SSD_GROUPS * SSD_STATE + g * SSD_STATE
        cg = xall[:, c0:c0 + SSD_STATE]
        cb = lax.dot_general(cg, bg, (((1,), (1,)), ((), ())), preferred_element_type=F32)
        bgt = bg.astype(F32).T
        cgf = cg.astype(F32)
        for pr in range(heads_per_group // 2):
            pair = g * (heads_per_group // 2) + pr
            xp = xall[:, pair * LANES:(pair + 1) * LANES]
            sp = s_ref[pair]
            spb = sp.astype(BF16)
            ys, us, decs = [], [], []
            for r in range(2):
                ln = 8 * d + 2 * pair + r
                colb = cum_col[:, ln:ln + 1]
                rowb = cum_row[ln:ln + 1, :]
                dtr = dtt[ln:ln + 1, :]
                tot_h = tot[ln:ln + 1, :]
                lmat = jnp.exp(jnp.where(mask, colb - rowb, -jnp.inf))
                w = (cb * lmat * dtr).astype(BF16)
                e = (cgf * jnp.exp(colb)).astype(BF16)
                ys.append(jnp.dot(w, xp, preferred_element_type=F32)
                          + jnp.dot(e, spb, preferred_element_type=F32))
                wrow = dtr * jnp.exp(tot_h - rowb)
                us.append(jnp.dot((bgt * wrow).astype(BF16), xp, preferred_element_type=F32))
                decs.append(jnp.exp(tot_h))
            y_ref[:, pair * LANES:(pair + 1) * LANES] = jnp.where(low_half, ys[0], ys[1])
            dec = jnp.where(low_half, decs[0], decs[1])
            s_ref[pair] = sp * dec + jnp.where(low_half, us[0], us[1])


def _ssd_kernel(xf_ref, dtf_ref, dttf_ref, xb_ref, dtb_ref, dttb_ref, arow_ref, acol_ref,
                yf_ref, yb_ref, s_ref):
    @pl.when(pl.program_id(0) == 0)
    def _():
        s_ref[...] = jnp.zeros_like(s_ref)

    for b in range(xf_ref.shape[0]):
        _ssd_direction(0, xf_ref.at[b], dtf_ref.at[b], dttf_ref.at[b], arow_ref, acol_ref,
                       s_ref.at[b, 0], yf_ref.at[b])
        _ssd_direction(1, xb_ref.at[b], dtb_ref.at[b], dttb_ref.at[b], arow_ref, acol_ref,
                       s_ref.at[b, 1], yb_ref.at[b])


def _ssd(xc, dt, dtt, arow, acol):
    bsz, t, ch = xc.shape
    nc = t // SSD_CHUNK
    nctx = CTX // SSD_CHUNK

    def fwd(c):
        return c

    def bwd(c):
        return jnp.where(c < nctx, nctx - 1 - c, nc - 1 + nctx - c)

    def specs(order):
        return [pl.BlockSpec((bsz, SSD_CHUNK, ch), lambda c: (0, order(c), 0)),
                pl.BlockSpec((bsz, SSD_CHUNK, LANES), lambda c: (0, order(c), 0)),
                pl.BlockSpec((bsz, 2 * SSD_HEADS, SSD_CHUNK), lambda c: (0, 0, order(c)))]

    y_spec = lambda order: pl.BlockSpec((bsz, SSD_CHUNK, SSD_WIDTH),
                                        lambda c: (0, order(c), 0))
    y_shape = jax.ShapeDtypeStruct((bsz, t, SSD_WIDTH), F32)
    return pl.pallas_call(
        _ssd_kernel,
        grid=(nc,),
        in_specs=specs(fwd) + specs(bwd) + [_const_spec((1, LANES)),
                                            _const_spec((2 * SSD_HEADS, 1))],
        out_specs=[y_spec(fwd), y_spec(bwd)],
        out_shape=[y_shape, y_shape],
        scratch_shapes=[pltpu.VMEM((bsz, 2, SSD_HEADS // 2, SSD_STATE, LANES), F32)],
        compiler_params=_cparams(1),
        name="ssd_scan",
    )(xc, dt, dtt, xc, dt, dtt, arow, acol)


_NT = (((1,), (1,)), ((), ()))


def _masked_q(q):
    lane = lax.broadcasted_iota(jnp.int32, (1, LANES), 1)
    comp0 = lane < DA_HEAD_DIM
    zero = jnp.zeros_like(q)
    return jnp.where(comp0, q, zero), jnp.where(comp0, zero, q)


def _diff_out(lam_init, acc0, acc1, lv_ref, sg_ref):
    lv = lv_ref[...]
    lam = (jnp.exp(jnp.sum(lv[0:1] * lv[1:2], axis=1, keepdims=True))
           - jnp.exp(jnp.sum(lv[2:3] * lv[3:4], axis=1, keepdims=True)) + lam_init)
    o = acc0[:, :LANES] / acc0[:, LANES:] - lam * (acc1[:, :LANES] / acc1[:, LANES:])
    return (_rms(o, sg_ref[...]) * (1.0 - lam_init)).astype(BF16)


def _attn_ctx_kernel(lam_init, q_ref, k_ref, v_ref, lv_ref, sg_ref, o_ref):
    kc = k_ref[...]
    vext = jnp.concatenate([v_ref[...], jnp.ones((CTX, LANES), BF16)], axis=1)
    accs = []
    for qm in _masked_q(q_ref[...]):
        s = lax.dot_general(qm, kc, _NT, preferred_element_type=F32)
        p = jnp.exp2(s - jnp.max(s, axis=1, keepdims=True)).astype(BF16)
        accs.append(jnp.dot(p, vext, preferred_element_type=F32))
    o_ref[...] = _diff_out(lam_init, accs[0], accs[1], lv_ref, sg_ref)


def _attn_lat_kernel(lam_init, qa_ref, qb_ref, k_ref, v_ref, lv_ref, sg_ref, o_ref,
                     vext_ref, acc_ref, m_ref, qm_ref, p_ref, al_ref, *s_bufs):
    t = k_ref.shape[0]
    n = t // ATT_TK
    assert t % ATT_TK == 0 and n >= 2 and len(s_bufs) == ATT_SLOTS

    @pl.when(pl.program_id(2) == 0)
    def _():
        vext_ref[:, :LANES] = v_ref[...]
        vext_ref[:, LANES:] = jnp.ones((t, LANES), BF16)

    half = ATT_TQ // 2
    for c, (qa, qb) in enumerate(zip(_masked_q(qa_ref[...]), _masked_q(qb_ref[...]))):
        qm_ref[c * ATT_TQ:c * ATT_TQ + half, :] = qa
        qm_ref[c * ATT_TQ + half:(c + 1) * ATT_TQ, :] = qb
    acc_ref[...] = jnp.zeros_like(acc_ref)
    m_ref[...] = jnp.full(m_ref.shape, NEG_BIG, F32)

    key_tiles = ATT_TK // ATT_KT
    rows_all = 2 * ATT_TQ
    nseg = rows_all // ATT_PV_ROWS

    def aligned(start):
        return start if isinstance(start, int) else pl.multiple_of(start, LANES)

    def scores_piece(j, slot, kt):
        kc = k_ref[pl.ds(aligned(j * ATT_TK + kt * ATT_KT), ATT_KT), :]
        s_bufs[slot][:, kt * ATT_KT:(kt + 1) * ATT_KT] = lax.dot_general(
            qm_ref[...], kc, _NT, preferred_element_type=F32)

    def softmax_piece(slot, r):
        rows = slice(r, r + ATT_RB)
        s = s_bufs[slot][rows, :]
        m_prev = m_ref[rows, :]
        m_next = jnp.maximum(m_prev, jnp.max(s, axis=1, keepdims=True))
        al_ref[rows, :] = jnp.exp2(m_prev - m_next)
        m_ref[rows, :] = m_next
        m_wide = jnp.concatenate([m_next] * (ATT_TK // LANES), axis=1)
        p_ref[rows, :] = jnp.exp2(s - m_wide).astype(BF16)

    def weighted_piece(j, seg):
        rows = slice(seg * ATT_PV_ROWS, (seg + 1) * ATT_PV_ROWS)
        vc = vext_ref[pl.ds(aligned(j * ATT_TK), ATT_TK), :]
        pv = jnp.dot(p_ref[rows, :], vc, preferred_element_type=F32)
        al = al_ref[rows, :]
        acc_ref[rows, :] = acc_ref[rows, :] * jnp.concatenate([al, al], axis=1) + pv

    def trip(j, slot, scores=True):
        for seg in range(nseg):
            for r in range(seg * ATT_PV_ROWS, (seg + 1) * ATT_PV_ROWS, ATT_RB):
                softmax_piece(slot, r)
            if scores:
                for kt in range(seg * key_tiles // nseg, (seg + 1) * key_tiles // nseg):
                    scores_piece(j + 1, (slot + 1) % ATT_SLOTS, kt)
            weighted_piece(j, seg)

    for kt in range(key_tiles):
        scores_piece(0, 0, kt)
    assert ATT_UNROLL % ATT_SLOTS == 0
    peel = (n - 1) % ATT_UNROLL
    for j in range(peel):
        trip(j, j % ATT_SLOTS)

    def body(i, carry):
        for u in range(ATT_UNROLL):
            trip(ATT_UNROLL * i + peel + u, (peel + u) % ATT_SLOTS)
        return carry

    if (n - 1) // ATT_UNROLL == 1:
        body(0, 0)
    else:
        lax.fori_loop(0, (n - 1) // ATT_UNROLL, body, 0)
    trip(n - 1, (n - 1) % ATT_SLOTS, scores=False)
    o_ref[...] = _diff_out(lam_init, acc_ref[:ATT_TQ, :], acc_ref[ATT_TQ:, :], lv_ref, sg_ref)


def _attention(q, k, v, lam_vec, sub_g, lam_init):
    bsz, t, _ = q.shape
    seq = t - CTX
    part = functools.partial
    consts = [_const_spec((4, DA_HEAD_DIM)), _const_spec((1, DA_V_DIM))]
    ctx_blk = pl.BlockSpec((None, CTX, LANES), lambda b, h: (b, 0, h))
    o_ctx = pl.pallas_call(
        part(_attn_ctx_kernel, lam_init),
        grid=(bsz, DA_HEADS),
        in_specs=[ctx_blk, ctx_blk, ctx_blk] + consts,
        out_specs=ctx_blk,
        out_shape=jax.ShapeDtypeStruct((bsz, CTX, DA_WIDTH), BF16),
        compiler_params=_cparams(2),
        name="diff_attn_ctx",
    )(q, k, v, lam_vec, sub_g)

    assert ATT_TQ == 2 * CTX
    kv = pl.BlockSpec((None, t, LANES), lambda b, h, i: (b, 0, h),
                      pipeline_mode=pl.Buffered(1))
    qa = pl.BlockSpec((None, CTX, LANES), lambda b, h, i: (b, 2 * i + 1, h))
    qb = pl.BlockSpec((None, CTX, LANES), lambda b, h, i: (b, 2 * i + 2, h))
    s_buf = pltpu.VMEM((2 * ATT_TQ, ATT_TK), F32)
    p_buf = pltpu.VMEM((2 * ATT_TQ, ATT_TK), BF16)
    al_buf = pltpu.VMEM((2 * ATT_TQ, LANES), F32)
    o_lat = pl.pallas_call(
        part(_attn_lat_kernel, lam_init),
        grid=(bsz, DA_HEADS, seq // ATT_TQ),
        in_specs=[qa, qb, kv, kv] + consts,
        out_specs=pl.BlockSpec((None, ATT_TQ, LANES), lambda b, h, i: (b, i, h)),
        out_shape=jax.ShapeDtypeStruct((bsz, seq, DA_WIDTH), BF16),
        scratch_shapes=[pltpu.VMEM((t, 2 * LANES), BF16),
                        pltpu.VMEM((2 * ATT_TQ, 2 * LANES), F32),
                        pltpu.VMEM((2 * ATT_TQ, LANES), F32),
                        pltpu.VMEM((2 * ATT_TQ, LANES), BF16), p_buf, al_buf]
                       + [s_buf] * ATT_SLOTS,
        compiler_params=_cparams(3),
        name="diff_attn",
    )(q, q, k, v, lam_vec, sub_g)
    return o_ctx, o_lat


def _mixout_kernel(x_ref, actx_ref, *refs):
    alat_refs = refs[:MIX_BLOCKS]
    (yf_ref, yb_ref, xc_ref, z_ref, dsk_ref, sg_ref, g_ref, gate_ref, wa_ref, ws_ref,
     o_ref) = refs[MIX_BLOCKS:]
    first_step = pl.program_id(1) == 0
    for k in range(MIX_BLOCKS):
        rows = slice(k * CTX, (k + 1) * CTX)
        attn = alat_refs[k][...]
        gate = gate_ref[0:1, :]
        if k == 0:
            attn = jnp.where(first_step, actx_ref[...], attn)
            gate = jnp.where(first_step, gate_ref[1:2, :], gate)
        y = yf_ref[rows, :] + yb_ref[rows, :] + dsk_ref[...] * xc_ref[rows, :].astype(F32)
        u = y * _silu(z_ref[rows, :].astype(F32))
        ssd = _rms(u, sg_ref[...]).astype(BF16)
        mix = (jnp.dot(attn, wa_ref[...], preferred_element_type=F32)
               + jnp.dot(ssd, ws_ref[...], preferred_element_type=F32))
        o_ref[rows, :] = x_ref[rows, :] + gate * _rms(mix, g_ref[...])


def _mixout(xx, attn_ctx, attn_lat, yf, yb, xc, z, dsk, ssd_g, g, gate, wa, ws):
    bsz, t, _ = xx.shape
    tile = MIX_BLOCKS * CTX
    assert t % tile == 0
    row = lambda w: pl.BlockSpec((None, tile, w), lambda b, i: (b, i, 0))
    mod = pl.BlockSpec((None, 2, D_MODEL), lambda b, i: (b, 0, 0))

    def lat_block(k):
        return pl.BlockSpec((None, CTX, DA_WIDTH),
                            lambda b, i: (b, jnp.maximum(MIX_BLOCKS * i + k - 1, 0), 0))

    return pl.pallas_call(
        _mixout_kernel,
        grid=(bsz, t // tile),
        in_specs=[row(D_MODEL), pl.BlockSpec((None, CTX, DA_WIDTH), lambda b, i: (b, 0, 0))]
                 + [lat_block(k) for k in range(MIX_BLOCKS)]
                 + [row(SSD_WIDTH), row(SSD_WIDTH),
                    row(SSD_WIDTH),
                    row(SSD_WIDTH), _const_spec((1, SSD_WIDTH)), _const_spec((1, SSD_WIDTH)),
                    _const_spec((1, D_MODEL)), mod, _const_spec(wa.shape),
                    _const_spec(ws.shape)],
        out_specs=row(D_MODEL),
        out_shape=jax.ShapeDtypeStruct(xx.shape, F32),
        compiler_params=_cparams(2),
        name="mix_out",
    )(xx, attn_ctx, *([attn_lat] * MIX_BLOCKS), yf, yb, xc, z, dsk, ssd_g, g, gate, wa, ws)


def _ffn_kernel(x_ref, gin_ref, sh_ref, sc_ref, gout_ref, gate_ref, wg_ref, wu_ref, wo_ref,
                o_ref):
    tile = x_ref.shape[0]
    is_ctx_all = _row_is_ctx(pl.program_id(1), tile)
    for r0 in range(0, tile, FFN_ROWS):
        rows = slice(r0, r0 + FFN_ROWS)
        is_ctx = is_ctx_all[rows, :]
        pick = lambda r: jnp.where(is_ctx, r[1:2, :], r[0:1, :])
        x = x_ref[rows, :]
        h = (_rms(x, gin_ref[...]) * (1.0 + pick(sc_ref)) + pick(sh_ref)).astype(BF16)
        acc = None
        for c in range(D_FF // FFN_CHUNK):
            sl = slice(c * FFN_CHUNK, (c + 1) * FFN_CHUNK)
            gt = jnp.dot(h, wg_ref[:, sl], preferred_element_type=F32)
            up = jnp.dot(h, wu_ref[:, sl], preferred_element_type=F32)
            a = (_silu(gt) * up).astype(BF16)
            part = jnp.dot(a, wo_ref[sl, :], preferred_element_type=F32)
            acc = part if acc is None else acc + part
        o_ref[rows, :] = x + pick(gate_ref) * _rms(acc, gout_ref[...])


def _ffn(xx, gin, sh, sc, gout, gate, wg, wu, wo):
    bsz, t, _ = xx.shape
    nt = t // ROW_TILE
    row = pl.BlockSpec((None, ROW_TILE, D_MODEL), lambda b, i: (b, i, 0))
    mod = pl.BlockSpec((None, 2, D_MODEL), lambda b, i: (b, 0, 0))
    return pl.pallas_call(
        _ffn_kernel,
        grid=(bsz, nt),
        in_specs=[row, _const_spec((1, D_MODEL)), mod, mod, _const_spec((1, D_MODEL)), mod,
                  _const_spec(wg.shape), _const_spec(wu.shape), _const_spec(wo.shape)],
        out_specs=row,
        out_shape=jax.ShapeDtypeStruct(xx.shape, F32),
        compiler_params=_cparams(2),
        name="ffn",
    )(xx, gin, sh, sc, gout, gate, wg, wu, wo)


def _rope_tables(seq):
    rows = seq // GRID_W
    row = jnp.repeat(jnp.arange(rows, dtype=F32), GRID_W)
    col = jnp.tile(jnp.arange(GRID_W, dtype=F32), rows)
    nf = DA_HEAD_DIM // 4
    inv = ROPE_BASE ** (-jnp.arange(nf, dtype=F32) / nf)
    ang = jnp.concatenate([row[:, None] * inv, col[:, None] * inv], axis=-1)
    cos = jnp.concatenate([jnp.ones((CTX, DA_HEAD_DIM // 2), F32), jnp.cos(ang)], axis=0)
    sin = jnp.concatenate([jnp.zeros((CTX, DA_HEAD_DIM // 2), F32), jnp.sin(ang)], axis=0)
    return jnp.tile(cos, (1, 4)), jnp.tile(jnp.concatenate([-sin, sin], axis=1), (1, 2))


def kernel(x, c, ctx, c_ctx, w_ada, b_ada, norm_g, w_in, conv_w, conv_b, a_log, dt_bias,
           d_skip, ssd_norm_g, diff_lambda, subln_g, w_out, w_ffn_in, w_ffn_out):
    bsz, seq, _ = x.shape
    depth = w_ada.shape[0]
    t = CTX + seq
    assert ctx.shape[1] == CTX and t % ROW_TILE == 0 and seq % ATT_TQ == 0

    xx = jnp.concatenate([ctx, x], axis=1)
    cond = jnp.zeros((8, D_MODEL), F32).at[:bsz].set(c).at[bsz].set(c_ctx)
    mod = _modulation(cond, w_ada, b_ada)
    cos_t, sin_t = _rope_tables(seq)

    q_end = DA_WIDTH
    k_end = 2 * DA_WIDTH
    v_end = 3 * DA_WIDTH
    z_end = v_end + SSD_WIDTH
    xbc_end = z_end + SSD_CONV_CH

    for i in range(depth):
        lam_init = 0.8 - 0.6 * math.exp(-0.3 * i)
        mods = [jnp.stack([mod[i, :bsz, j * D_MODEL:(j + 1) * D_MODEL],
                           jnp.broadcast_to(mod[i, bsz, j * D_MODEL:(j + 1) * D_MODEL],
                                            (bsz, D_MODEL))], axis=1) for j in range(N_MOD)]
        sh1, sc1, g1, sh2, sc2, g2 = mods
        wi = w_in[i].astype(BF16)
        wdt = jnp.zeros((D_MODEL, LANES), BF16).at[:, :2 * SSD_HEADS].set(wi[:, xbc_end:])
        lane16 = lambda a: jnp.zeros((1, LANES), F32).at[0, :2 * SSD_HEADS].set(a.reshape(-1))
        q, k, v, z, xbc, dt, dtt = _inproj(
            xx, norm_g[i, 0][None], sh1, sc1, cos_t, sin_t, lane16(dt_bias[i]),
            wi[:, :q_end], wi[:, q_end:k_end], wi[:, k_end:v_end], wi[:, v_end:z_end],
            wi[:, z_end:xbc_end], wdt)
        cw = jnp.zeros((8, SSD_CONV_CH), F32).at[:SSD_CONV].set(conv_w[i])
        xc = _conv(xbc, cw, conv_b[i][None])
        yf, yb = _ssd(xc, dt, dtt, lane16(a_log[i]), a_log[i].reshape(2 * SSD_HEADS, 1))
        attn_ctx, attn_lat = _attention(q, k, v, diff_lambda[i], subln_g[i][None], lam_init)
        wo = w_out[i].astype(BF16)
        xx = _mixout(xx, attn_ctx, attn_lat, yf, yb, xc, z,
                     jnp.repeat(d_skip[i], SSD_HEAD_DIM)[None], ssd_norm_g[i][None],
                     norm_g[i, 1][None], g1, wo[:DA_WIDTH], wo[DA_WIDTH:])
        wf = w_ffn_in[i].astype(BF16)
        xx = _ffn(xx, norm_g[i, 2][None], sh2, sc2, norm_g[i, 3][None], g2,
                  wf[:, :D_FF], wf[:, D_FF:], w_ffn_out[i].astype(BF16))
    return xx[:, CTX:]
```

```python
import functools
import math

import jax
import jax.numpy as jnp
from jax import lax
from jax.experimental import pallas as pl
from jax.experimental.pallas import tpu as pltpu

F32 = jnp.float32
BF16 = jnp.bfloat16

D_MODEL = 1024
N_MOD = 6
CTX = 256
GRID_W = 64
DA_HEADS = 4
DA_HEAD_DIM = 64
DA_V_DIM = 2 * DA_HEAD_DIM
DA_WIDTH = DA_HEADS * DA_V_DIM
SSD_HEADS = 8
SSD_HEAD_DIM = 64
SSD_WIDTH = SSD_HEADS * SSD_HEAD_DIM
SSD_GROUPS = 2
SSD_STATE = 128
SSD_CONV = 5
SSD_CHUNK = 128
SSD_CONV_CH = SSD_WIDTH + 2 * SSD_GROUPS * SSD_STATE
D_FF = 2816
ROPE_BASE = 10000.0
EPS = 1e-6

LANES = 128
SUBLANES = 8
MOD_TILE = 1536
ROW_TILE = 640
CONV_TILE = 256
CONV_HALO = 16
ATT_TQ = 512
ATT_TK = 1280
ATT_SLOTS = 3
ATT_TAIL = 256
ATT_RB = 16
ATT_KT = 256
ATT_PV_ROWS = 256
FFN_CHUNK = 1408
FFN_ROWS = 320
INPROJ_ROWS = 384
MIX_BLOCKS = 5
VMEM_LIMIT = 56 * 1024 * 1024
LOG2E = 1.4426950408889634
NEG_BIG = -1e30


def _cparams(n_axes):
    return pltpu.CompilerParams(
        dimension_semantics=("arbitrary",) * n_axes, vmem_limit_bytes=VMEM_LIMIT)


def _rms(u, g):
    return u * lax.rsqrt(jnp.mean(u * u, axis=-1, keepdims=True) + EPS) * g


def _silu(u):
    return u * jax.nn.sigmoid(u)


def _row_is_ctx(step, tile):
    rows = step * tile + lax.broadcasted_iota(jnp.int32, (tile, 1), 0)
    return rows < CTX


_NT = (((1,), (1,)), ((), ()))


def _const_spec(shape):
    zeros = (0,) * len(shape)
    return pl.BlockSpec(shape, lambda *_: zeros, pipeline_mode=pl.Buffered(1))


def _mod_kernel(c_ref, w_ref, b_ref, o_ref):
    s = _silu(c_ref[...])
    o_ref[...] = jnp.dot(s, w_ref[...], precision=lax.Precision.HIGHEST,
                         preferred_element_type=F32) + b_ref[...]


def _modulation(cond, w_ada, b_ada):
    depth = w_ada.shape[0]
    tn = MOD_TILE
    return pl.pallas_call(
        _mod_kernel,
        grid=(depth, N_MOD * D_MODEL // tn),
        in_specs=[pl.BlockSpec((SUBLANES, D_MODEL), lambda i, j: (0, 0)),
                  pl.BlockSpec((None, D_MODEL, tn), lambda i, j: (i, 0, j)),
                  pl.BlockSpec((None, 1, tn), lambda i, j: (i, 0, j))],
        out_specs=pl.BlockSpec((None, SUBLANES, tn), lambda i, j: (i, 0, j)),
        out_shape=jax.ShapeDtypeStruct((depth, SUBLANES, N_MOD * D_MODEL), F32),
        compiler_params=_cparams(2),
        name="adaln_mod",
    )(cond, w_ada, b_ada.reshape(depth, 1, N_MOD * D_MODEL))


def _inproj_kernel(x_ref, g_ref, sh_ref, sc_ref, cos_ref, sin_ref, dtb_ref,
                   wq_ref, wk_ref, wv_ref, wz_ref, wx_ref, wdt_ref,
                   q_ref, k_ref, v_ref, z_ref, xbc_ref, dt_ref, dtt_ref):
    tile = x_ref.shape[0]
    is_ctx_all = _row_is_ctx(pl.program_id(1), tile)
    lane = lax.broadcasted_iota(jnp.int32, (1, LANES), 1)
    first_half = (lane % DA_HEAD_DIM) < (DA_HEAD_DIM // 2)
    q_scale = DA_HEAD_DIM ** -0.5 * LOG2E

    def rope(u, rows):
        partner = jnp.where(first_half,
                            pltpu.roll(u, LANES - DA_HEAD_DIM // 2, axis=1),
                            pltpu.roll(u, DA_HEAD_DIM // 2, axis=1))
        return u * cos_ref[rows, :] + partner * sin_ref[rows, :]

    blocks = [slice(r0, min(r0 + INPROJ_ROWS, tile)) for r0 in range(0, tile, INPROJ_ROWS)]
    hs = []
    for rows in blocks:
        is_ctx = is_ctx_all[rows, :]
        shift = jnp.where(is_ctx, sh_ref[1:2, :], sh_ref[0:1, :])
        scale = jnp.where(is_ctx, sc_ref[1:2, :], sc_ref[0:1, :])
        hs.append((_rms(x_ref[rows, :], g_ref[...]) * (1.0 + scale) + shift).astype(BF16))
    for rows, h in zip(blocks, hs):
        q = jnp.dot(h, wq_ref[...], preferred_element_type=F32)
        for hd in range(DA_HEADS):
            sl = slice(hd * LANES, (hd + 1) * LANES)
            q_ref[rows, sl] = (rope(q[:, sl], rows) * q_scale).astype(BF16)
    for rows, h in zip(blocks, hs):
        k = jnp.dot(h, wk_ref[...], preferred_element_type=F32)
        for hd in range(DA_HEADS):
            sl = slice(hd * LANES, (hd + 1) * LANES)
            k_ref[rows, sl] = rope(k[:, sl], rows).astype(BF16)
    for w_ref, out_ref in ((wv_ref, v_ref), (wz_ref, z_ref), (wx_ref, xbc_ref)):
        for rows, h in zip(blocks, hs):
            out_ref[rows, :] = jnp.dot(h, w_ref[...], preferred_element_type=F32).astype(BF16)
    for rows, h in zip(blocks, hs):
        dt_raw = jnp.dot(h, wdt_ref[...], preferred_element_type=F32) + dtb_ref[...]
        dt = jnp.maximum(dt_raw, 0.0) + jnp.log1p(jnp.exp(-jnp.abs(dt_raw)))
        dt_ref[rows, :] = dt
        dtt_ref[:, rows] = dt.T[:2 * SSD_HEADS, :]


def _inproj(xx, g, sh, sc, cos_t, sin_t, dtb, wq, wk, wv, wz, wx, wdt):
    bsz, t, _ = xx.shape
    nt = t // ROW_TILE
    row = lambda w: pl.BlockSpec((None, ROW_TILE, w), lambda b, i: (b, i, 0))
    mod = pl.BlockSpec((None, 2, D_MODEL), lambda b, i: (b, 0, 0))
    tab = pl.BlockSpec((ROW_TILE, LANES), lambda b, i: (i, 0))
    return pl.pallas_call(
        _inproj_kernel,
        grid=(bsz, nt),
        in_specs=[row(D_MODEL), _const_spec((1, D_MODEL)), mod, mod, tab, tab,
                  _const_spec((1, LANES)),
                  _const_spec(wq.shape), _const_spec(wk.shape), _const_spec(wv.shape),
                  _const_spec(wz.shape), _const_spec(wx.shape), _const_spec(wdt.shape)],
        out_specs=[row(DA_WIDTH), row(DA_WIDTH), row(DA_WIDTH), row(SSD_WIDTH),
                   row(SSD_CONV_CH), row(LANES),
                   pl.BlockSpec((None, 2 * SSD_HEADS, ROW_TILE), lambda b, i: (b, 0, i))],
        out_shape=[jax.ShapeDtypeStruct((bsz, t, DA_WIDTH), BF16),
                   jax.ShapeDtypeStruct((bsz, t, DA_WIDTH), BF16),
                   jax.ShapeDtypeStruct((bsz, t, DA_WIDTH), BF16),
                   jax.ShapeDtypeStruct((bsz, t, SSD_WIDTH), BF16),
                   jax.ShapeDtypeStruct((bsz, t, SSD_CONV_CH), BF16),
                   jax.ShapeDtypeStruct((bsz, t, LANES), F32),
                   jax.ShapeDtypeStruct((bsz, 2 * SSD_HEADS, t), F32)],
        compiler_params=_cparams(2),
        name="inproj",
    )(xx, g, sh, sc, cos_t, sin_t, dtb, wq, wk, wv, wz, wx, wdt)


def _conv_kernel(cur_ref, prev_ref, next_ref, w_ref, b_ref, o_ref, ext_ref):
    i = pl.program_id(1)
    n = pl.num_programs(1)
    half = SSD_CONV // 2
    pad = SUBLANES
    first_of_segment = jnp.logical_or(i == 0, i == CTX // CONV_TILE)
    last_of_segment = jnp.logical_or(i == CTX // CONV_TILE - 1, i == n - 1)
    prev = prev_ref[...].astype(F32)[CONV_HALO - pad:, :]
    nxt = next_ref[...].astype(F32)[:pad, :]
    ext_ref[0:pad, :] = jnp.where(first_of_segment, 0.0, prev)
    ext_ref[pad:pad + CONV_TILE, :] = cur_ref[...].astype(F32)
    ext_ref[pad + CONV_TILE:, :] = jnp.where(last_of_segment, 0.0, nxt)
    acc = b_ref[...] + w_ref[half:half + 1, :] * ext_ref[pad:pad + CONV_TILE, :]
    for s in range(-half, half + 1):
        if s != 0:
            acc = acc + w_ref[half + s:half + s + 1, :] * ext_ref[pl.ds(pad + s, CONV_TILE), :]
    o_ref[...] = _silu(acc).astype(BF16)


def _conv(xbc, w, b):
    bsz, t, ch = xbc.shape
    nt = t // CONV_TILE
    per = CONV_TILE // CONV_HALO
    last_halo = t // CONV_HALO - 1
    return pl.pallas_call(
        _conv_kernel,
        grid=(bsz, nt),
        in_specs=[pl.BlockSpec((None, CONV_TILE, ch), lambda b_, i: (b_, i, 0)),
                  pl.BlockSpec((None, CONV_HALO, ch),
                               lambda b_, i: (b_, jnp.maximum(i * per - 1, 0), 0)),
                  pl.BlockSpec((None, CONV_HALO, ch),
                               lambda b_, i: (b_, jnp.minimum((i + 1) * per, last_halo), 0)),
                  _const_spec((SUBLANES, ch)), _const_spec((1, ch))],
        out_specs=pl.BlockSpec((None, CONV_TILE, ch), lambda b_, i: (b_, i, 0)),
        out_shape=jax.ShapeDtypeStruct((bsz, t, ch), BF16),
        scratch_shapes=[pltpu.VMEM((CONV_TILE + 2 * SUBLANES, ch), F32)],
        compiler_params=_cparams(2),
        name="dwconv",
    )(xbc, xbc, xbc, w, b)


def _ssd_chains(chains, arow_ref, acol_ref):
    q = SSD_CHUNK
    hi = lax.Precision.HIGHEST
    ii = lax.broadcasted_iota(jnp.int32, (q, q), 0)
    jj = lax.broadcasted_iota(jnp.int32, (q, q), 1)
    lane = lax.broadcasted_iota(jnp.int32, (1, LANES), 1)
    low_half = lane < SSD_HEAD_DIM
    a_row = -jnp.exp(arow_ref[...])
    a_col = -jnp.exp(acol_ref[...])
    heads_per_group = SSD_HEADS // SSD_GROUPS
    st = []
    for d, xc_ref, dt_ref, dtt_ref, _, _ in chains:
        dtt = dtt_ref[...]
        dat = dtt * a_col
        mask = (jj <= ii) if d == 0 else (jj >= ii)
        mf = mask.astype(F32)
        st.append(dict(
            xall=xc_ref[...], dtt=dtt, mask=mask,
            cum_col=jnp.dot(mf, dt_ref[...] * a_row, precision=hi,
                            preferred_element_type=F32),
            cum_row=lax.dot_general(dat, mf, _NT, precision=hi,
                                    preferred_element_type=F32),
            tot=jnp.sum(dat, axis=1, keepdims=True)))
    for g in range(SSD_GROUPS):
        for c in st:
            bg = c["xall"][:, SSD_WIDTH + g * SSD_STATE:SSD_WIDTH + (g + 1) * SSD_STATE]
            c0 = SSD_WIDTH + SSD_GROUPS * SSD_STATE + g * SSD_STATE
            cg = c["xall"][:, c0:c0 + SSD_STATE]
            c["cb"] = lax.dot_general(cg, bg, _NT, preferred_element_type=F32)
            c["bgt"] = bg.astype(F32).T
            c["cgf"] = cg.astype(F32)
        for pr in range(heads_per_group // 2):
            pair = g * (heads_per_group // 2) + pr
            for c, (_, _, _, _, s_ref, _) in zip(st, chains):
                c["xp"] = c["xall"][:, pair * LANES:(pair + 1) * LANES]
                c["sp"] = s_ref[pair]
                c["spb"] = c["sp"].astype(BF16)
                c["ys"], c["us"], c["decs"] = [], [], []
            for r in range(2):
                for c, (d, *_) in zip(st, chains):
                    ln = SSD_HEADS * d + 2 * pair + r
                    colb = c["cum_col"][:, ln:ln + 1]
                    rowb = c["cum_row"][ln:ln + 1, :]
                    dtr = c["dtt"][ln:ln + 1, :]
                    tot_h = c["tot"][ln:ln + 1, :]
                    lmat = jnp.exp(jnp.where(c["mask"], colb - rowb, -jnp.inf))
                    w = (c["cb"] * lmat * dtr).astype(BF16)
                    e = (c["cgf"] * jnp.exp(colb)).astype(BF16)
                    c["ys"].append(jnp.dot(w, c["xp"], preferred_element_type=F32)
                                   + jnp.dot(e, c["spb"], preferred_element_type=F32))
                    wrow = dtr * jnp.exp(tot_h - rowb)
                    c["us"].append(jnp.dot((c["bgt"] * wrow).astype(BF16), c["xp"],
                                           preferred_element_type=F32))
                    c["decs"].append(jnp.exp(tot_h))
            for c, (_, _, _, _, s_ref, y_ref) in zip(st, chains):
                y_ref[:, pair * LANES:(pair + 1) * LANES] = jnp.where(
                    low_half, c["ys"][0], c["ys"][1])
                dec = jnp.where(low_half, c["decs"][0], c["decs"][1])
                s_ref[pair] = c["sp"] * dec + jnp.where(low_half, c["us"][0], c["us"][1])


def _ssd_kernel(xf_ref, dtf_ref, dttf_ref, xb_ref, dtb_ref, dttb_ref, arow_ref, acol_ref,
                yf_ref, yb_ref, s_ref):
    @pl.when(pl.program_id(0) == 0)
    def _():
        s_ref[...] = jnp.zeros_like(s_ref)

    chains = []
    for b in range(xf_ref.shape[0]):
        chains.append((0, xf_ref.at[b], dtf_ref.at[b], dttf_ref.at[b], s_ref.at[b, 0],
                       yf_ref.at[b]))
        chains.append((1, xb_ref.at[b], dtb_ref.at[b], dttb_ref.at[b], s_ref.at[b, 1],
                       yb_ref.at[b]))
    _ssd_chains(chains, arow_ref, acol_ref)


def _ssd(xc, dt, dtt, arow, acol):
    bsz, t, ch = xc.shape
    nc = t // SSD_CHUNK
    nctx = CTX // SSD_CHUNK

    def fwd(c):
        return c

    def bwd(c):
        return jnp.where(c < nctx, nctx - 1 - c, nc - 1 + nctx - c)

    def specs(order):
        return [pl.BlockSpec((bsz, SSD_CHUNK, ch), lambda c: (0, order(c), 0)),
                pl.BlockSpec((bsz, SSD_CHUNK, LANES), lambda c: (0, order(c), 0)),
                pl.BlockSpec((bsz, 2 * SSD_HEADS, SSD_CHUNK), lambda c: (0, 0, order(c)))]

    y_spec = lambda order: pl.BlockSpec((bsz, SSD_CHUNK, SSD_WIDTH),
                                        lambda c: (0, order(c), 0))
    y_shape = jax.ShapeDtypeStruct((bsz, t, SSD_WIDTH), F32)
    return pl.pallas_call(
        _ssd_kernel,
        grid=(nc,),
        in_specs=specs(fwd) + specs(bwd) + [_const_spec((1, LANES)),
                                            _const_spec((2 * SSD_HEADS, 1))],
        out_specs=[y_spec(fwd), y_spec(bwd)],
        out_shape=[y_shape, y_shape],
        scratch_shapes=[pltpu.VMEM((bsz, 2, SSD_HEADS // 2, SSD_STATE, LANES), F32)],
        compiler_params=_cparams(1),
        name="ssd_scan",
    )(xc, dt, dtt, xc, dt, dtt, arow, acol)


def _masked_q(q):
    lane = lax.broadcasted_iota(jnp.int32, (1, LANES), 1)
    comp0 = lane < DA_HEAD_DIM
    zero = jnp.zeros_like(q)
    return jnp.where(comp0, q, zero), jnp.where(comp0, zero, q)


def _diff_out(lam_init, acc0, acc1, lv_ref, sg_ref):
    lv = lv_ref[...]
    lam = (jnp.exp(jnp.sum(lv[0:1] * lv[1:2], axis=1, keepdims=True))
           - jnp.exp(jnp.sum(lv[2:3] * lv[3:4], axis=1, keepdims=True)) + lam_init)
    o = acc0[:, :LANES] / acc0[:, LANES:] - lam * (acc1[:, :LANES] / acc1[:, LANES:])
    return (_rms(o, sg_ref[...]) * (1.0 - lam_init)).astype(BF16)


def _attn_ctx_kernel(lam_init, q_ref, k_ref, v_ref, lv_ref, sg_ref, o_ref):
    kc = k_ref[...]
    vext = jnp.concatenate([v_ref[...], jnp.ones((CTX, LANES), BF16)], axis=1)
    accs = []
    for qm in _masked_q(q_ref[...]):
        s = lax.dot_general(qm, kc, _NT, preferred_element_type=F32)
        p = jnp.exp2(s - jnp.max(s, axis=1, keepdims=True)).astype(BF16)
        accs.append(jnp.dot(p, vext, preferred_element_type=F32))
    o_ref[...] = _diff_out(lam_init, accs[0], accs[1], lv_ref, sg_ref)


def _attn_lat_kernel(lam_init, qa_ref, qb_ref, qa_next_ref, qb_next_ref, k_ref, v_ref, lv_ref,
                     sg_ref, o_ref, vext_ref, acc_ref, m_ref, qm_ref, qn_ref, p_ref, al_ref,
                     *s_bufs):
    t = k_ref.shape[0]
    assert t % ATT_TK == 0 and ATT_TAIL % ATT_KT == 0 and len(s_bufs) == ATT_SLOTS
    first_block = pl.program_id(2) == 0

    @pl.when(first_block)
    def _():
        vext_ref[:, :LANES] = v_ref[...]
        vext_ref[:, LANES:] = jnp.ones((t, LANES), BF16)

    def stack_masked(dst_ref, a_ref, b_ref):
        half = ATT_TQ // 2
        for c, (qa, qb) in enumerate(zip(_masked_q(a_ref[...]), _masked_q(b_ref[...]))):
            dst_ref[c * ATT_TQ:c * ATT_TQ + half, :] = qa
            dst_ref[c * ATT_TQ + half:(c + 1) * ATT_TQ, :] = qb

    stack_masked(qm_ref, qa_ref, qb_ref)
    acc_ref[...] = jnp.zeros_like(acc_ref)
    m_ref[...] = jnp.full(m_ref.shape, NEG_BIG, F32)

    bounds = list(range(0, t, ATT_TK)) + [t - ATT_TAIL, t]
    n = len(bounds) - 1
    rows_all = 2 * ATT_TQ
    nseg = rows_all // ATT_PV_ROWS

    def key_tiles(j):
        return (bounds[j + 1] - bounds[j]) // ATT_KT if j < n else 0

    def scores_piece(j, kt, q_ref=qm_ref):
        start = bounds[j] + kt * ATT_KT
        s_bufs[j % ATT_SLOTS][:, kt * ATT_KT:(kt + 1) * ATT_KT] = lax.dot_general(
            q_ref[...], k_ref[start:start + ATT_KT, :], _NT, preferred_element_type=F32)

    def softmax_piece(j, r):
        rows = slice(r, r + ATT_RB)
        size = bounds[j + 1] - bounds[j]
        s = s_bufs[j % ATT_SLOTS][rows, :size]
        m_prev = m_ref[rows, :]
        m_next = jnp.maximum(m_prev, jnp.max(s, axis=1, keepdims=True))
        al_ref[rows, :] = jnp.exp2(m_prev - m_next)
        m_ref[rows, :] = m_next
        m_wide = jnp.concatenate([m_next] * (size // LANES), axis=1)
        p_ref[rows, :size] = jnp.exp2(s - m_wide).astype(BF16)

    def weighted_piece(j, seg):
        rows = slice(seg * ATT_PV_ROWS, (seg + 1) * ATT_PV_ROWS)
        size = bounds[j + 1] - bounds[j]
        pv = jnp.dot(p_ref[rows, :size], vext_ref[bounds[j]:bounds[j + 1], :],
                     preferred_element_type=F32)
        al = al_ref[rows, :]
        acc_ref[rows, :] = acc_ref[rows, :] * jnp.concatenate([al, al], axis=1) + pv

    @pl.when(first_block)
    def _():
        for kt in range(key_tiles(0)):
            scores_piece(0, kt)

    for j in range(n):
        last = j == n - 1
        if last:
            stack_masked(qn_ref, qa_next_ref, qb_next_ref)
        tiles_next = key_tiles(0) if last else key_tiles(j + 1)
        for seg in range(nseg):
            for r in range(seg * ATT_PV_ROWS, (seg + 1) * ATT_PV_ROWS, ATT_RB):
                softmax_piece(j, r)
            for kt in range(seg * tiles_next // nseg, (seg + 1) * tiles_next // nseg):
                if last:
                    scores_piece(0, kt, qn_ref)
                else:
                    scores_piece(j + 1, kt)
            weighted_piece(j, seg)
    o_ref[...] = _diff_out(lam_init, acc_ref[:ATT_TQ, :], acc_ref[ATT_TQ:, :], lv_ref, sg_ref)


def _attention(q, k, v, lam_vec, sub_g, lam_init):
    bsz, t, _ = q.shape
    seq = t - CTX
    part = functools.partial
    consts = [_const_spec((4, DA_HEAD_DIM)), _const_spec((1, DA_V_DIM))]
    ctx_blk = pl.BlockSpec((None, CTX, LANES), lambda b, h: (b, 0, h))
    o_ctx = pl.pallas_call(
        part(_attn_ctx_kernel, lam_init),
        grid=(bsz, DA_HEADS),
        in_specs=[ctx_blk, ctx_blk, ctx_blk] + consts,
        out_specs=ctx_blk,
        out_shape=jax.ShapeDtypeStruct((bsz, CTX, DA_WIDTH), BF16),
        compiler_params=_cparams(2),
        name="diff_attn_ctx",
    )(q, k, v, lam_vec, sub_g)

    assert ATT_TQ == 2 * CTX
    kv = pl.BlockSpec((None, t, LANES), lambda b, h, i: (b, 0, h),
                      pipeline_mode=pl.Buffered(1))
    nq = seq // ATT_TQ

    def q_half(offset, ahead):
        return pl.BlockSpec(
            (None, CTX, LANES),
            lambda b, h, i: (b, 2 * jnp.minimum(i + ahead, nq - 1) + offset, h))

    s_buf = pltpu.VMEM((2 * ATT_TQ, ATT_TK), F32)
    p_buf = pltpu.VMEM((2 * ATT_TQ, ATT_TK), BF16)
    al_buf = pltpu.VMEM((2 * ATT_TQ, LANES), F32)
    o_lat = pl.pallas_call(
        part(_attn_lat_kernel, lam_init),
        grid=(bsz, DA_HEADS, nq),
        in_specs=[q_half(1, 0), q_half(2, 0), q_half(1, 1), q_half(2, 1), kv, kv] + consts,
        out_specs=pl.BlockSpec((None, ATT_TQ, LANES), lambda b, h, i: (b, i, h)),
        out_shape=jax.ShapeDtypeStruct((bsz, seq, DA_WIDTH), BF16),
        scratch_shapes=[pltpu.VMEM((t, 2 * LANES), BF16),
                        pltpu.VMEM((2 * ATT_TQ, 2 * LANES), F32),
                        pltpu.VMEM((2 * ATT_TQ, LANES), F32),
                        pltpu.VMEM((2 * ATT_TQ, LANES), BF16),
                        pltpu.VMEM((2 * ATT_TQ, LANES), BF16), p_buf, al_buf]
                       + [s_buf] * ATT_SLOTS,
        compiler_params=_cparams(3),
        name="diff_attn",
    )(q, q, q, q, k, v, lam_vec, sub_g)
    return o_ctx, o_lat


def _mixout_kernel(x_ref, actx_ref, *refs):
    alat_refs = refs[:MIX_BLOCKS]
    (yf_ref, yb_ref, xc_ref, z_ref, dsk_ref, sg_ref, g_ref, gate_ref, wa_ref, ws_ref,
     o_ref) = refs[MIX_BLOCKS:]
    first_step = pl.program_id(1) == 0
    for k in range(MIX_BLOCKS):
        rows = slice(k * CTX, (k + 1) * CTX)
        attn = alat_refs[k][...]
        gate = gate_ref[0:1, :]
        if k == 0:
            attn = jnp.where(first_step, actx_ref[...], attn)
            gate = jnp.where(first_step, gate_ref[1:2, :], gate)
        y = yf_ref[rows, :] + yb_ref[rows, :] + dsk_ref[...] * xc_ref[rows, :].astype(F32)
        u = y * _silu(z_ref[rows, :].astype(F32))
        ssd = _rms(u, sg_ref[...]).astype(BF16)
        mix = (jnp.dot(attn, wa_ref[...], preferred_element_type=F32)
               + jnp.dot(ssd, ws_ref[...], preferred_element_type=F32))
        o_ref[rows, :] = x_ref[rows, :] + gate * _rms(mix, g_ref[...])


def _mixout(xx, attn_ctx, attn_lat, yf, yb, xc, z, dsk, ssd_g, g, gate, wa, ws):
    bsz, t, _ = xx.shape
    tile = MIX_BLOCKS * CTX
    assert t % tile == 0
    row = lambda w: pl.BlockSpec((None, tile, w), lambda b, i: (b, i, 0))
    mod = pl.BlockSpec((None, 2, D_MODEL), lambda b, i: (b, 0, 0))

    def lat_block(k):
        return pl.BlockSpec((None, CTX, DA_WIDTH),
                            lambda b, i: (b, jnp.maximum(MIX_BLOCKS * i + k - 1, 0), 0))

    return pl.pallas_call(
        _mixout_kernel,
        grid=(bsz, t // tile),
        in_specs=[row(D_MODEL), pl.BlockSpec((None, CTX, DA_WIDTH), lambda b, i: (b, 0, 0))]
                 + [lat_block(k) for k in range(MIX_BLOCKS)]
                 + [row(SSD_WIDTH), row(SSD_WIDTH),
                    row(SSD_WIDTH),
                    row(SSD_WIDTH), _const_spec((1, SSD_WIDTH)), _const_spec((1, SSD_WIDTH)),
                    _const_spec((1, D_MODEL)), mod, _const_spec(wa.shape),
                    _const_spec(ws.shape)],
        out_specs=row(D_MODEL),
        out_shape=jax.ShapeDtypeStruct(xx.shape, F32),
        compiler_params=_cparams(2),
        name="mix_out",
    )(xx, attn_ctx, *([attn_lat] * MIX_BLOCKS), yf, yb, xc, z, dsk, ssd_g, g, gate, wa, ws)


def _ffn_kernel(x_ref, gin_ref, sh_ref, sc_ref, gout_ref, gate_ref, wg_ref, wu_ref, wo_ref,
                o_ref):
    tile = x_ref.shape[0]
    is_ctx_all = _row_is_ctx(pl.program_id(1), tile)
    blocks = [slice(r0, r0 + FFN_ROWS) for r0 in range(0, tile, FFN_ROWS)]
    pick = lambda r, rows: jnp.where(is_ctx_all[rows, :], r[1:2, :], r[0:1, :])
    hs = [(_rms(x_ref[rows, :], gin_ref[...]) * (1.0 + pick(sc_ref, rows))
           + pick(sh_ref, rows)).astype(BF16) for rows in blocks]
    accs = [None] * len(blocks)
    for c in range(D_FF // FFN_CHUNK):
        sl = slice(c * FFN_CHUNK, (c + 1) * FFN_CHUNK)
        for b, h in enumerate(hs):
            gt = jnp.dot(h, wg_ref[:, sl], preferred_element_type=F32)
            up = jnp.dot(h, wu_ref[:, sl], preferred_element_type=F32)
            a = (_silu(gt) * up).astype(BF16)
            part = jnp.dot(a, wo_ref[sl, :], preferred_element_type=F32)
            accs[b] = part if accs[b] is None else accs[b] + part
    for rows, acc in zip(blocks, accs):
        o_ref[rows, :] = x_ref[rows, :] + pick(gate_ref, rows) * _rms(acc, gout_ref[...])


def _ffn(xx, gin, sh, sc, gout, gate, wg, wu, wo):
    bsz, t, _ = xx.shape
    nt = t // ROW_TILE
    row = pl.BlockSpec((None, ROW_TILE, D_MODEL), lambda b, i: (b, i, 0))
    mod = pl.BlockSpec((None, 2, D_MODEL), lambda b, i: (b, 0, 0))
    return pl.pallas_call(
        _ffn_kernel,
        grid=(bsz, nt),
        in_specs=[row, _const_spec((1, D_MODEL)), mod, mod, _const_spec((1, D_MODEL)), mod,
                  _const_spec(wg.shape), _const_spec(wu.shape), _const_spec(wo.shape)],
        out_specs=row,
        out_shape=jax.ShapeDtypeStruct(xx.shape, F32),
        compiler_params=_cparams(2),
        name="ffn",
    )(xx, gin, sh, sc, gout, gate, wg, wu, wo)


def _rope_tables(seq):
    rows = seq // GRID_W
    row = jnp.repeat(jnp.arange(rows, dtype=F32), GRID_W)
    col = jnp.tile(jnp.arange(GRID_W, dtype=F32), rows)
    nf = DA_HEAD_DIM // 4
    inv = ROPE_BASE ** (-jnp.arange(nf, dtype=F32) / nf)
    ang = jnp.concatenate([row[:, None] * inv, col[:, None] * inv], axis=-1)
    cos = jnp.concatenate([jnp.ones((CTX, DA_HEAD_DIM // 2), F32), jnp.cos(ang)], axis=0)
    sin = jnp.concatenate([jnp.zeros((CTX, DA_HEAD_DIM // 2), F32), jnp.sin(ang)], axis=0)
    return jnp.tile(cos, (1, 4)), jnp.tile(jnp.concatenate([-sin, sin], axis=1), (1, 2))


def kernel(x, c, ctx, c_ctx, w_ada, b_ada, norm_g, w_in, conv_w, conv_b, a_log, dt_bias,
           d_skip, ssd_norm_g, diff_lambda, subln_g, w_out, w_ffn_in, w_ffn_out):
    bsz, seq, _ = x.shape
    depth = w_ada.shape[0]
    t = CTX + seq
    assert ctx.shape[1] == CTX and t % ROW_TILE == 0 and seq % ATT_TQ == 0

    xx = jnp.concatenate([ctx, x], axis=1)
    cond = jnp.zeros((SUBLANES, D_MODEL), F32).at[:bsz].set(c).at[bsz].set(c_ctx)
    mod = _modulation(cond, w_ada, b_ada)
    cos_t, sin_t = _rope_tables(seq)

    q_end = DA_WIDTH
    k_end = 2 * DA_WIDTH
    v_end = 3 * DA_WIDTH
    z_end = v_end + SSD_WIDTH
    xbc_end = z_end + SSD_CONV_CH

    for i in range(depth):
        lam_init = 0.8 - 0.6 * math.exp(-0.3 * i)
        mods = [jnp.stack([mod[i, :bsz, j * D_MODEL:(j + 1) * D_MODEL],
                           jnp.broadcast_to(mod[i, bsz, j * D_MODEL:(j + 1) * D_MODEL],
                                            (bsz, D_MODEL))], axis=1) for j in range(N_MOD)]
        sh1, sc1, g1, sh2, sc2, g2 = mods
        wi = w_in[i].astype(BF16)
        wdt = jnp.zeros((D_MODEL, LANES), BF16).at[:, :2 * SSD_HEADS].set(wi[:, xbc_end:])
        lane16 = lambda a: jnp.zeros((1, LANES), F32).at[0, :2 * SSD_HEADS].set(a.reshape(-1))
        q, k, v, z, xbc, dt, dtt = _inproj(
            xx, norm_g[i, 0][None], sh1, sc1, cos_t, sin_t, lane16(dt_bias[i]),
            wi[:, :q_end], wi[:, q_end:k_end], wi[:, k_end:v_end], wi[:, v_end:z_end],
            wi[:, z_end:xbc_end], wdt)
        cw = jnp.zeros((SUBLANES, SSD_CONV_CH), F32).at[:SSD_CONV].set(conv_w[i])
        xc = _conv(xbc, cw, conv_b[i][None])
        yf, yb = _ssd(xc, dt, dtt, lane16(a_log[i]), a_log[i].reshape(2 * SSD_HEADS, 1))
        attn_ctx, attn_lat = _attention(q, k, v, diff_lambda[i], subln_g[i][None], lam_init)
        wo = w_out[i].astype(BF16)
        xx = _mixout(xx, attn_ctx, attn_lat, yf, yb, xc, z,
                     jnp.repeat(d_skip[i], SSD_HEAD_DIM)[None], ssd_norm_g[i][None],
                     norm_g[i, 1][None], g1, wo[:DA_WIDTH], wo[DA_WIDTH:])
        wf = w_ffn_in[i].astype(BF16)
        xx = _ffn(xx, norm_g[i, 2][None], sh2, sc2, norm_g[i, 3][None], g2,
                  wf[:, :D_FF], wf[:, D_FF:], w_ffn_out[i].astype(BF16))
    return xx[:, CTX:]
```

```python
import functools
import math

import jax
import jax.numpy as jnp
from jax import lax
from jax.experimental import pallas as pl
from jax.experimental.pallas import tpu as pltpu

F32 = jnp.float32
BF16 = jnp.bfloat16

D_MODEL = 1024
N_MOD = 6
CTX = 256
GRID_W = 64
DA_HEADS = 4
DA_HEAD_DIM = 64
DA_V_DIM = 2 * DA_HEAD_DIM
DA_WIDTH = DA_HEADS * DA_V_DIM
SSD_HEADS = 8
SSD_HEAD_DIM = 64
SSD_WIDTH = SSD_HEADS * SSD_HEAD_DIM
SSD_GROUPS = 2
SSD_STATE = 128
SSD_CONV = 5
SSD_CHUNK = 128
SSD_CONV_CH = SSD_WIDTH + 2 * SSD_GROUPS * SSD_STATE
D_FF = 2816
ROPE_BASE = 10000.0
EPS = 1e-6

LANES = 128
SUBLANES = 8
MOD_TILE = 1536
ROW_TILE = 640
CONV_TILE = 256
CONV_HALO = 16
ATT_TQ = 512
ATT_TK = 1280
ATT_SLOTS = 3
ATT_TAIL = 256
ATT_RB = 16
ATT_KT = 256
ATT_PV_ROWS = 512
FFN_CHUNK = 1408
FFN_ROWS = 320
INPROJ_ROWS = 384
MIX_BLOCKS = 5
VMEM_LIMIT = 56 * 1024 * 1024
LOG2E = 1.4426950408889634
NEG_BIG = -1e30


def _cparams(n_axes):
    return pltpu.CompilerParams(
        dimension_semantics=("arbitrary",) * n_axes, vmem_limit_bytes=VMEM_LIMIT)


def _rms(u, g):
    return u * lax.rsqrt(jnp.mean(u * u, axis=-1, keepdims=True) + EPS) * g


def _silu(u):
    return u * jax.nn.sigmoid(u)


def _row_is_ctx(step, tile):
    rows = step * tile + lax.broadcasted_iota(jnp.int32, (tile, 1), 0)
    return rows < CTX


_NT = (((1,), (1,)), ((), ()))


def _const_spec(shape):
    zeros = (0,) * len(shape)
    return pl.BlockSpec(shape, lambda *_: zeros, pipeline_mode=pl.Buffered(1))


def _mod_kernel(c_ref, w_ref, b_ref, o_ref):
    s = _silu(c_ref[...])
    o_ref[...] = jnp.dot(s, w_ref[...], precision=lax.Precision.HIGHEST,
                         preferred_element_type=F32) + b_ref[...]


def _modulation(cond, w_ada, b_ada):
    depth = w_ada.shape[0]
    tn = MOD_TILE
    return pl.pallas_call(
        _mod_kernel,
        grid=(depth, N_MOD * D_MODEL // tn),
        in_specs=[pl.BlockSpec((SUBLANES, D_MODEL), lambda i, j: (0, 0)),
                  pl.BlockSpec((None, D_MODEL, tn), lambda i, j: (i, 0, j)),
                  pl.BlockSpec((None, 1, tn), lambda i, j: (i, 0, j))],
        out_specs=pl.BlockSpec((None, SUBLANES, tn), lambda i, j: (i, 0, j)),
        out_shape=jax.ShapeDtypeStruct((depth, SUBLANES, N_MOD * D_MODEL), F32),
        compiler_params=_cparams(2),
        name="adaln_mod",
    )(cond, w_ada, b_ada.reshape(depth, 1, N_MOD * D_MODEL))


def _inproj_kernel(x_ref, g_ref, sh_ref, sc_ref, cos_ref, sin_ref, dtb_ref,
                   wq_ref, wk_ref, wv_ref, wz_ref, wx_ref, wdt_ref,
                   q_ref, k_ref, v_ref, z_ref, xbc_ref, dt_ref, dtt_ref):
    tile = x_ref.shape[0]
    is_ctx_all = _row_is_ctx(pl.program_id(1), tile)
    lane = lax.broadcasted_iota(jnp.int32, (1, LANES), 1)
    first_half = (lane % DA_HEAD_DIM) < (DA_HEAD_DIM // 2)
    q_scale = DA_HEAD_DIM ** -0.5 * LOG2E

    def rope(u, rows):
        partner = jnp.where(first_half,
                            pltpu.roll(u, LANES - DA_HEAD_DIM // 2, axis=1),
                            pltpu.roll(u, DA_HEAD_DIM // 2, axis=1))
        return u * cos_ref[rows, :] + partner * sin_ref[rows, :]

    blocks = [slice(r0, min(r0 + INPROJ_ROWS, tile)) for r0 in range(0, tile, INPROJ_ROWS)]
    hs = []
    for rows in blocks:
        is_ctx = is_ctx_all[rows, :]
        shift = jnp.where(is_ctx, sh_ref[1:2, :], sh_ref[0:1, :])
        scale = jnp.where(is_ctx, sc_ref[1:2, :], sc_ref[0:1, :])
        hs.append((_rms(x_ref[rows, :], g_ref[...]) * (1.0 + scale) + shift).astype(BF16))
    for rows, h in zip(blocks, hs):
        q = jnp.dot(h, wq_ref[...], preferred_element_type=F32)
        for hd in range(DA_HEADS):
            sl = slice(hd * LANES, (hd + 1) * LANES)
            q_ref[rows, sl] = (rope(q[:, sl], rows) * q_scale).astype(BF16)
    for rows, h in zip(blocks, hs):
        k = jnp.dot(h, wk_ref[...], preferred_element_type=F32)
        for hd in range(DA_HEADS):
            sl = slice(hd * LANES, (hd + 1) * LANES)
            k_ref[rows, sl] = rope(k[:, sl], rows).astype(BF16)
    for w_ref, out_ref in ((wv_ref, v_ref), (wz_ref, z_ref), (wx_ref, xbc_ref)):
        for rows, h in zip(blocks, hs):
            out_ref[rows, :] = jnp.dot(h, w_ref[...], preferred_element_type=F32).astype(BF16)
    for rows, h in zip(blocks, hs):
        dt_raw = jnp.dot(h, wdt_ref[...], preferred_element_type=F32) + dtb_ref[...]
        dt = jnp.maximum(dt_raw, 0.0) + jnp.log1p(jnp.exp(-jnp.abs(dt_raw)))
        dt_ref[rows, :] = dt
        dtt_ref[:, rows] = dt.T[:2 * SSD_HEADS, :]


def _inproj(xx, g, sh, sc, cos_t, sin_t, dtb, wq, wk, wv, wz, wx, wdt):
    bsz, t, _ = xx.shape
    nt = t // ROW_TILE
    row = lambda w: pl.BlockSpec((None, ROW_TILE, w), lambda b, i: (b, i, 0))
    mod = pl.BlockSpec((None, 2, D_MODEL), lambda b, i: (b, 0, 0))
    tab = pl.BlockSpec((ROW_TILE, LANES), lambda b, i: (i, 0))
    return pl.pallas_call(
        _inproj_kernel,
        grid=(bsz, nt),
        in_specs=[row(D_MODEL), _const_spec((1, D_MODEL)), mod, mod, tab, tab,
                  _const_spec((1, LANES)),
                  _const_spec(wq.shape), _const_spec(wk.shape), _const_spec(wv.shape),
                  _const_spec(wz.shape), _const_spec(wx.shape), _const_spec(wdt.shape)],
        out_specs=[row(DA_WIDTH), row(DA_WIDTH), row(DA_WIDTH), row(SSD_WIDTH),
                   row(SSD_CONV_CH), row(LANES),
                   pl.BlockSpec((None, 2 * SSD_HEADS, ROW_TILE), lambda b, i: (b, 0, i))],
        out_shape=[jax.ShapeDtypeStruct((bsz, t, DA_WIDTH), BF16),
                   jax.ShapeDtypeStruct((bsz, t, DA_WIDTH), BF16),
                   jax.ShapeDtypeStruct((bsz, t, DA_WIDTH), BF16),
                   jax.ShapeDtypeStruct((bsz, t, SSD_WIDTH), BF16),
                   jax.ShapeDtypeStruct((bsz, t, SSD_CONV_CH), BF16),
                   jax.ShapeDtypeStruct((bsz, t, LANES), F32),
                   jax.ShapeDtypeStruct((bsz, 2 * SSD_HEADS, t), F32)],
        compiler_params=_cparams(2),
        name="inproj",
    )(xx, g, sh, sc, cos_t, sin_t, dtb, wq, wk, wv, wz, wx, wdt)


def _conv_kernel(cur_ref, prev_ref, next_ref, w_ref, b_ref, o_ref, ext_ref):
    i = pl.program_id(1)
    n = pl.num_programs(1)
    half = SSD_CONV // 2
    pad = SUBLANES
    first_of_segment = jnp.logical_or(i == 0, i == CTX // CONV_TILE)
    last_of_segment = jnp.logical_or(i == CTX // CONV_TILE - 1, i == n - 1)
    prev = prev_ref[...].astype(F32)[CONV_HALO - pad:, :]
    nxt = next_ref[...].astype(F32)[:pad, :]
    ext_ref[0:pad, :] = jnp.where(first_of_segment, 0.0, prev)
    ext_ref[pad:pad + CONV_TILE, :] = cur_ref[...].astype(F32)
    ext_ref[pad + CONV_TILE:, :] = jnp.where(last_of_segment, 0.0, nxt)
    acc = b_ref[...] + w_ref[half:half + 1, :] * ext_ref[pad:pad + CONV_TILE, :]
    for s in range(-half, half + 1):
        if s != 0:
            acc = acc + w_ref[half + s:half + s + 1, :] * ext_ref[pl.ds(pad + s, CONV_TILE), :]
    o_ref[...] = _silu(acc).astype(BF16)


def _conv(xbc, w, b):
    bsz, t, ch = xbc.shape
    nt = t // CONV_TILE
    per = CONV_TILE // CONV_HALO
    last_halo = t // CONV_HALO - 1
    return pl.pallas_call(
        _conv_kernel,
        grid=(bsz, nt),
        in_specs=[pl.BlockSpec((None, CONV_TILE, ch), lambda b_, i: (b_, i, 0)),
                  pl.BlockSpec((None, CONV_HALO, ch),
                               lambda b_, i: (b_, jnp.maximum(i * per - 1, 0), 0)),
                  pl.BlockSpec((None, CONV_HALO, ch),
                               lambda b_, i: (b_, jnp.minimum((i + 1) * per, last_halo), 0)),
                  _const_spec((SUBLANES, ch)), _const_spec((1, ch))],
        out_specs=pl.BlockSpec((None, CONV_TILE, ch), lambda b_, i: (b_, i, 0)),
        out_shape=jax.ShapeDtypeStruct((bsz, t, ch), BF16),
        scratch_shapes=[pltpu.VMEM((CONV_TILE + 2 * SUBLANES, ch), F32)],
        compiler_params=_cparams(2),
        name="dwconv",
    )(xbc, xbc, xbc, w, b)


def _ssd_chains(chains, arow_ref, acol_ref):
    q = SSD_CHUNK
    hi = lax.Precision.HIGHEST
    ii = lax.broadcasted_iota(jnp.int32, (q, q), 0)
    jj = lax.broadcasted_iota(jnp.int32, (q, q), 1)
    lane = lax.broadcasted_iota(jnp.int32, (1, LANES), 1)
    low_half = lane < SSD_HEAD_DIM
    a_row = -jnp.exp(arow_ref[...])
    a_col = -jnp.exp(acol_ref[...])
    heads_per_group = SSD_HEADS // SSD_GROUPS
    st = []
    for d, xc_ref, dt_ref, dtt_ref, _, _ in chains:
        dtt = dtt_ref[...]
        dat = dtt * a_col
        mask = (jj <= ii) if d == 0 else (jj >= ii)
        mf = mask.astype(F32)
        st.append(dict(
            xall=xc_ref[...], dtt=dtt, mask=mask,
            cum_col=jnp.dot(mf, dt_ref[...] * a_row, precision=hi,
                            preferred_element_type=F32),
            cum_row=lax.dot_general(dat, mf, _NT, precision=hi,
                                    preferred_element_type=F32),
            tot=jnp.sum(dat, axis=1, keepdims=True)))
    for g in range(SSD_GROUPS):
        for c in st:
            bg = c["xall"][:, SSD_WIDTH + g * SSD_STATE:SSD_WIDTH + (g + 1) * SSD_STATE]
            c0 = SSD_WIDTH + SSD_GROUPS * SSD_STATE + g * SSD_STATE
            cg = c["xall"][:, c0:c0 + SSD_STATE]
            c["cb"] = lax.dot_general(cg, bg, _NT, preferred_element_type=F32)
            c["bgt"] = bg.astype(F32).T
            c["cgf"] = cg.astype(F32)
        for pr in range(heads_per_group // 2):
            pair = g * (heads_per_group // 2) + pr
            for c, (_, _, _, _, s_ref, _) in zip(st, chains):
                c["xp"] = c["xall"][:, pair * LANES:(pair + 1) * LANES]
                c["sp"] = s_ref[pair]
                c["spb"] = c["sp"].astype(BF16)
                c["ys"], c["us"], c["decs"] = [], [], []
            for r in range(2):
                for c, (d, *_) in zip(st, chains):
                    ln = SSD_HEADS * d + 2 * pair + r
                    colb = c["cum_col"][:, ln:ln + 1]
                    rowb = c["cum_row"][ln:ln + 1, :]
                    dtr = c["dtt"][ln:ln + 1, :]
                    tot_h = c["tot"][ln:ln + 1, :]
                    lmat = jnp.exp(jnp.where(c["mask"], colb - rowb, -jnp.inf))
                    w = (c["cb"] * lmat * dtr).astype(BF16)
                    e = (c["cgf"] * jnp.exp(colb)).astype(BF16)
                    c["ys"].append(jnp.dot(w, c["xp"], preferred_element_type=F32)
                                   + jnp.dot(e, c["spb"], preferred_element_type=F32))
                    wrow = dtr * jnp.exp(tot_h - rowb)
                    c["us"].append(jnp.dot((c["bgt"] * wrow).astype(BF16), c["xp"],
                                           preferred_element_type=F32))
                    c["decs"].append(jnp.exp(tot_h))
            for c, (_, _, _, _, s_ref, y_ref) in zip(st, chains):
                y_ref[:, pair * LANES:(pair + 1) * LANES] = jnp.where(
                    low_half, c["ys"][0], c["ys"][1])
                dec = jnp.where(low_half, c["decs"][0], c["decs"][1])
                s_ref[pair] = c["sp"] * dec + jnp.where(low_half, c["us"][0], c["us"][1])


def _ssd_kernel(xf_ref, dtf_ref, dttf_ref, xb_ref, dtb_ref, dttb_ref, arow_ref, acol_ref,
                yf_ref, yb_ref, s_ref):
    @pl.when(pl.program_id(0) == 0)
    def _():
        s_ref[...] = jnp.zeros_like(s_ref)

    chains = []
    for b in range(xf_ref.shape[0]):
        chains.append((0, xf_ref.at[b], dtf_ref.at[b], dttf_ref.at[b], s_ref.at[b, 0],
                       yf_ref.at[b]))
        chains.append((1, xb_ref.at[b], dtb_ref.at[b], dttb_ref.at[b], s_ref.at[b, 1],
                       yb_ref.at[b]))
    _ssd_chains(chains, arow_ref, acol_ref)


def _ssd(xc, dt, dtt, arow, acol):
    bsz, t, ch = xc.shape
    nc = t // SSD_CHUNK
    nctx = CTX // SSD_CHUNK

    def fwd(c):
        return c

    def bwd(c):
        return jnp.where(c < nctx, nctx - 1 - c, nc - 1 + nctx - c)

    def specs(order):
        return [pl.BlockSpec((bsz, SSD_CHUNK, ch), lambda c: (0, order(c), 0)),
                pl.BlockSpec((bsz, SSD_CHUNK, LANES), lambda c: (0, order(c), 0)),
                pl.BlockSpec((bsz, 2 * SSD_HEADS, SSD_CHUNK), lambda c: (0, 0, order(c)))]

    y_spec = lambda order: pl.BlockSpec((bsz, SSD_CHUNK, SSD_WIDTH),
                                        lambda c: (0, order(c), 0))
    y_shape = jax.ShapeDtypeStruct((bsz, t, SSD_WIDTH), F32)
    return pl.pallas_call(
        _ssd_kernel,
        grid=(nc,),
        in_specs=specs(fwd) + specs(bwd) + [_const_spec((1, LANES)),
                                            _const_spec((2 * SSD_HEADS, 1))],
        out_specs=[y_spec(fwd), y_spec(bwd)],
        out_shape=[y_shape, y_shape],
        scratch_shapes=[pltpu.VMEM((bsz, 2, SSD_HEADS // 2, SSD_STATE, LANES), F32)],
        compiler_params=_cparams(1),
        name="ssd_scan",
    )(xc, dt, dtt, xc, dt, dtt, arow, acol)


def _masked_q(q):
    lane = lax.broadcasted_iota(jnp.int32, (1, LANES), 1)
    comp0 = lane < DA_HEAD_DIM
    zero = jnp.zeros_like(q)
    return jnp.where(comp0, q, zero), jnp.where(comp0, zero, q)


def _diff_out(lam_init, acc0, acc1, lv_ref, sg_ref):
    lv = lv_ref[...]
    lam = (jnp.exp(jnp.sum(lv[0:1] * lv[1:2], axis=1, keepdims=True))
           - jnp.exp(jnp.sum(lv[2:3] * lv[3:4], axis=1, keepdims=True)) + lam_init)
    o = acc0[:, :LANES] / acc0[:, LANES:] - lam * (acc1[:, :LANES] / acc1[:, LANES:])
    return (_rms(o, sg_ref[...]) * (1.0 - lam_init)).astype(BF16)


def _attn_ctx_kernel(lam_init, q_ref, k_ref, v_ref, lv_ref, sg_ref, o_ref):
    kc = k_ref[...]
    vext = jnp.concatenate([v_ref[...], jnp.ones((CTX, LANES), BF16)], axis=1)
    accs = []
    for qm in _masked_q(q_ref[...]):
        s = lax.dot_general(qm, kc, _NT, preferred_element_type=F32)
        p = jnp.exp2(s - jnp.max(s, axis=1, keepdims=True)).astype(BF16)
        accs.append(jnp.dot(p, vext, preferred_element_type=F32))
    o_ref[...] = _diff_out(lam_init, accs[0], accs[1], lv_ref, sg_ref)


def _attn_lat_kernel(lam_init, qa_ref, qb_ref, k_ref, v_ref, lv_ref, sg_ref, o_ref,
                     vext_ref, acc_ref, m_ref, qm_ref, p_ref, al_ref, *s_bufs):
    t = k_ref.shape[0]
    assert t % ATT_TK == 0 and ATT_TAIL % ATT_KT == 0 and len(s_bufs) == ATT_SLOTS

    @pl.when(pl.program_id(2) == 0)
    def _():
        vext_ref[:, :LANES] = v_ref[...]
        vext_ref[:, LANES:] = jnp.ones((t, LANES), BF16)

    half = ATT_TQ // 2
    for c, (qa, qb) in enumerate(zip(_masked_q(qa_ref[...]), _masked_q(qb_ref[...]))):
        qm_ref[c * ATT_TQ:c * ATT_TQ + half, :] = qa
        qm_ref[c * ATT_TQ + half:(c + 1) * ATT_TQ, :] = qb
    acc_ref[...] = jnp.zeros_like(acc_ref)
    m_ref[...] = jnp.full(m_ref.shape, NEG_BIG, F32)

    bounds = list(range(0, t, ATT_TK)) + [t - ATT_TAIL, t]
    n = len(bounds) - 1
    rows_all = 2 * ATT_TQ
    nseg = rows_all // ATT_PV_ROWS

    def key_tiles(j):
        return (bounds[j + 1] - bounds[j]) // ATT_KT if j < n else 0

    def scores_piece(j, kt0, kt1):
        lo, hi = kt0 * ATT_KT, kt1 * ATT_KT
        s_bufs[j % ATT_SLOTS][:, lo:hi] = lax.dot_general(
            qm_ref[...], k_ref[bounds[j] + lo:bounds[j] + hi, :], _NT,
            preferred_element_type=F32)

    def softmax_piece(j, r):
        rows = slice(r, r + ATT_RB)
        size = bounds[j + 1] - bounds[j]
        s = s_bufs[j % ATT_SLOTS][rows, :size]
        m_prev = m_ref[rows, :]
        m_next = jnp.maximum(m_prev, jnp.max(s, axis=1, keepdims=True))
        al_ref[rows, :] = jnp.exp2(m_prev - m_next)
        m_ref[rows, :] = m_next
        m_wide = jnp.concatenate([m_next] * (size // LANES), axis=1)
        p_ref[rows, :size] = jnp.exp2(s - m_wide).astype(BF16)

    def weighted_piece(j, seg):
        rows = slice(seg * ATT_PV_ROWS, (seg + 1) * ATT_PV_ROWS)
        size = bounds[j + 1] - bounds[j]
        pv = jnp.dot(p_ref[rows, :size], vext_ref[bounds[j]:bounds[j + 1], :],
                     preferred_element_type=F32)
        al = al_ref[rows, :]
        acc_ref[rows, :] = acc_ref[rows, :] * jnp.concatenate([al, al], axis=1) + pv

    scores_piece(0, 0, key_tiles(0))
    for j in range(n):
        tiles_next = key_tiles(j + 1)
        for seg in range(nseg):
            for r in range(seg * ATT_PV_ROWS, (seg + 1) * ATT_PV_ROWS, ATT_RB):
                softmax_piece(j, r)
            kt0, kt1 = seg * tiles_next // nseg, (seg + 1) * tiles_next // nseg
            if kt1 > kt0:
                scores_piece(j + 1, kt0, kt1)
            weighted_piece(j, seg)
    o_ref[...] = _diff_out(lam_init, acc_ref[:ATT_TQ, :], acc_ref[ATT_TQ:, :], lv_ref, sg_ref)


def _attention(q, k, v, lam_vec, sub_g, lam_init):
    bsz, t, _ = q.shape
    seq = t - CTX
    part = functools.partial
    consts = [_const_spec((4, DA_HEAD_DIM)), _const_spec((1, DA_V_DIM))]
    ctx_blk = pl.BlockSpec((None, CTX, LANES), lambda b, h: (b, 0, h))
    o_ctx = pl.pallas_call(
        part(_attn_ctx_kernel, lam_init),
        grid=(bsz, DA_HEADS),
        in_specs=[ctx_blk, ctx_blk, ctx_blk] + consts,
        out_specs=ctx_blk,
        out_shape=jax.ShapeDtypeStruct((bsz, CTX, DA_WIDTH), BF16),
        compiler_params=_cparams(2),
        name="diff_attn_ctx",
    )(q, k, v, lam_vec, sub_g)

    assert ATT_TQ == 2 * CTX
    kv = pl.BlockSpec((None, t, LANES), lambda b, h, i: (b, 0, h),
                      pipeline_mode=pl.Buffered(1))
    qa = pl.BlockSpec((None, CTX, LANES), lambda b, h, i: (b, 2 * i + 1, h))
    qb = pl.BlockSpec((None, CTX, LANES), lambda b, h, i: (b, 2 * i + 2, h))
    s_buf = pltpu.VMEM((2 * ATT_TQ, ATT_TK), F32)
    p_buf = pltpu.VMEM((2 * ATT_TQ, ATT_TK), BF16)
    al_buf = pltpu.VMEM((2 * ATT_TQ, LANES), F32)
    o_lat = pl.pallas_call(
        part(_attn_lat_kernel, lam_init),
        grid=(bsz, DA_HEADS, seq // ATT_TQ),
        in_specs=[qa, qb, kv, kv] + consts,
        out_specs=pl.BlockSpec((None, ATT_TQ, LANES), lambda b, h, i: (b, i, h)),
        out_shape=jax.ShapeDtypeStruct((bsz, seq, DA_WIDTH), BF16),
        scratch_shapes=[pltpu.VMEM((t, 2 * LANES), BF16),
                        pltpu.VMEM((2 * ATT_TQ, 2 * LANES), F32),
                        pltpu.VMEM((2 * ATT_TQ, LANES), F32),
                        pltpu.VMEM((2 * ATT_TQ, LANES), BF16), p_buf, al_buf]
                       + [s_buf] * ATT_SLOTS,
        compiler_params=_cparams(3),
        name="diff_attn",
    )(q, q, k, v, lam_vec, sub_g)
    return o_ctx, o_lat


def _mixout_kernel(x_ref, actx_ref, *refs):
    alat_refs = refs[:MIX_BLOCKS]
    (yf_ref, yb_ref, xc_ref, z_ref, dsk_ref, sg_ref, g_ref, gate_ref, wa_ref, ws_ref,
     o_ref) = refs[MIX_BLOCKS:]
    first_step = pl.program_id(1) == 0
    for k in range(MIX_BLOCKS):
        rows = slice(k * CTX, (k + 1) * CTX)
        attn = alat_refs[k][...]
        gate = gate_ref[0:1, :]
        if k == 0:
            attn = jnp.where(first_step, actx_ref[...], attn)
            gate = jnp.where(first_step, gate_ref[1:2, :], gate)
        y = yf_ref[rows, :] + yb_ref[rows, :] + dsk_ref[...] * xc_ref[rows, :].astype(F32)
        u = y * _silu(z_ref[rows, :].astype(F32))
        ssd = _rms(u, sg_ref[...]).astype(BF16)
        mix = (jnp.dot(attn, wa_ref[...], preferred_element_type=F32)
               + jnp.dot(ssd, ws_ref[...], preferred_element_type=F32))
        o_ref[rows, :] = x_ref[rows, :] + gate * _rms(mix, g_ref[...])


def _mixout(xx, attn_ctx, attn_lat, yf, yb, xc, z, dsk, ssd_g, g, gate, wa, ws):
    bsz, t, _ = xx.shape
    tile = MIX_BLOCKS * CTX
    assert t % tile == 0
    row = lambda w: pl.BlockSpec((None, tile, w), lambda b, i: (b, i, 0))
    mod = pl.BlockSpec((None, 2, D_MODEL), lambda b, i: (b, 0, 0))

    def lat_block(k):
        return pl.BlockSpec((None, CTX, DA_WIDTH),
                            lambda b, i: (b, jnp.maximum(MIX_BLOCKS * i + k - 1, 0), 0))

    return pl.pallas_call(
        _mixout_kernel,
        grid=(bsz, t // tile),
        in_specs=[row(D_MODEL), pl.BlockSpec((None, CTX, DA_WIDTH), lambda b, i: (b, 0, 0))]
                 + [lat_block(k) for k in range(MIX_BLOCKS)]
                 + [row(SSD_WIDTH), row(SSD_WIDTH),
                    row(SSD_WIDTH),
                    row(SSD_WIDTH), _const_spec((1, SSD_WIDTH)), _const_spec((1, SSD_WIDTH)),
                    _const_spec((1, D_MODEL)), mod, _const_spec(wa.shape),
                    _const_spec(ws.shape)],
        out_specs=row(D_MODEL),
        out_shape=jax.ShapeDtypeStruct(xx.shape, F32),
        compiler_params=_cparams(2),
        name="mix_out",
    )(xx, attn_ctx, *([attn_lat] * MIX_BLOCKS), yf, yb, xc, z, dsk, ssd_g, g, gate, wa, ws)


def _ffn_kernel(x_ref, gin_ref, sh_ref, sc_ref, gout_ref, gate_ref, wg_ref, wu_ref, wo_ref,
                o_ref):
    tile = x_ref.shape[0]
    is_ctx_all = _row_is_ctx(pl.program_id(1), tile)
    blocks = [slice(r0, r0 + FFN_ROWS) for r0 in range(0, tile, FFN_ROWS)]
    pick = lambda r, rows: jnp.where(is_ctx_all[rows, :], r[1:2, :], r[0:1, :])
    hs = [(_rms(x_ref[rows, :], gin_ref[...]) * (1.0 + pick(sc_ref, rows))
           + pick(sh_ref, rows)).astype(BF16) for rows in blocks]
    accs = [None] * len(blocks)
    for c in range(D_FF // FFN_CHUNK):
        sl = slice(c * FFN_CHUNK, (c + 1) * FFN_CHUNK)
        for b, h in enumerate(hs):
            gt = jnp.dot(h, wg_ref[:, sl], preferred_element_type=F32)
            up = jnp.dot(h, wu_ref[:, sl], preferred_element_type=F32)
            a = (_silu(gt) * up).astype(BF16)
            part = jnp.dot(a, wo_ref[sl, :], preferred_element_type=F32)
            accs[b] = part if accs[b] is None else accs[b] + part
    for rows, acc in zip(blocks, accs):
        o_ref[rows, :] = x_ref[rows, :] + pick(gate_ref, rows) * _rms(acc, gout_ref[...])


def _ffn(xx, gin, sh, sc, gout, gate, wg, wu, wo):
    bsz, t, _ = xx.shape
    nt = t // ROW_TILE
    row = pl.BlockSpec((None, ROW_TILE, D_MODEL), lambda b, i: (b, i, 0))
    mod = pl.BlockSpec((None, 2, D_MODEL), lambda b, i: (b, 0, 0))
    return pl.pallas_call(
        _ffn_kernel,
        grid=(bsz, nt),
        in_specs=[row, _const_spec((1, D_MODEL)), mod, mod, _const_spec((1, D_MODEL)), mod,
                  _const_spec(wg.shape), _const_spec(wu.shape), _const_spec(wo.shape)],
        out_specs=row,
        out_shape=jax.ShapeDtypeStruct(xx.shape, F32),
        compiler_params=_cparams(2),
        name="ffn",
    )(xx, gin, sh, sc, gout, gate, wg, wu, wo)


def _rope_tables(seq):
    rows = seq // GRID_W
    row = jnp.repeat(jnp.arange(rows, dtype=F32), GRID_W)
    col = jnp.tile(jnp.arange(GRID_W, dtype=F32), rows)
    nf = DA_HEAD_DIM // 4
    inv = ROPE_BASE ** (-jnp.arange(nf, dtype=F32) / nf)
    ang = jnp.concatenate([row[:, None] * inv, col[:, None] * inv], axis=-1)
    cos = jnp.concatenate([jnp.ones((CTX, DA_HEAD_DIM // 2), F32), jnp.cos(ang)], axis=0)
    sin = jnp.concatenate([jnp.zeros((CTX, DA_HEAD_DIM // 2), F32), jnp.sin(ang)], axis=0)
    return jnp.tile(cos, (1, 4)), jnp.tile(jnp.concatenate([-sin, sin], axis=1), (1, 2))


def kernel(x, c, ctx, c_ctx, w_ada, b_ada, norm_g, w_in, conv_w, conv_b, a_log, dt_bias,
           d_skip, ssd_norm_g, diff_lambda, subln_g, w_out, w_ffn_in, w_ffn_out):
    bsz, seq, _ = x.shape
    depth = w_ada.shape[0]
    t = CTX + seq
    assert ctx.shape[1] == CTX and t % ROW_TILE == 0 and seq % ATT_TQ == 0

    xx = jnp.concatenate([ctx, x], axis=1)
    cond = jnp.zeros((SUBLANES, D_MODEL), F32).at[:bsz].set(c).at[bsz].set(c_ctx)
    mod = _modulation(cond, w_ada, b_ada)
    cos_t, sin_t = _rope_tables(seq)

    q_end = DA_WIDTH
    k_end = 2 * DA_WIDTH
    v_end = 3 * DA_WIDTH
    z_end = v_end + SSD_WIDTH
    xbc_end = z_end + SSD_CONV_CH

    for i in range(depth):
        lam_init = 0.8 - 0.6 * math.exp(-0.3 * i)
        mods = [jnp.stack([mod[i, :bsz, j * D_MODEL:(j + 1) * D_MODEL],
                           jnp.broadcast_to(mod[i, bsz, j * D_MODEL:(j + 1) * D_MODEL],
                                            (bsz, D_MODEL))], axis=1) for j in range(N_MOD)]
        sh1, sc1, g1, sh2, sc2, g2 = mods
        wi = w_in[i].astype(BF16)
        wdt = jnp.zeros((D_MODEL, LANES), BF16).at[:, :2 * SSD_HEADS].set(wi[:, xbc_end:])
        lane16 = lambda a: jnp.zeros((1, LANES), F32).at[0, :2 * SSD_HEADS].set(a.reshape(-1))
        q, k, v, z, xbc, dt, dtt = _inproj(
            xx, norm_g[i, 0][None], sh1, sc1, cos_t, sin_t, lane16(dt_bias[i]),
            wi[:, :q_end], wi[:, q_end:k_end], wi[:, k_end:v_end], wi[:, v_end:z_end],
            wi[:, z_end:xbc_end], wdt)
        cw = jnp.zeros((SUBLANES, SSD_CONV_CH), F32).at[:SSD_CONV].set(conv_w[i])
        xc = _conv(xbc, cw, conv_b[i][None])
        yf, yb = _ssd(xc, dt, dtt, lane16(a_log[i]), a_log[i].reshape(2 * SSD_HEADS, 1))
        attn_ctx, attn_lat = _attention(q, k, v, diff_lambda[i], subln_g[i][None], lam_init)
        wo = w_out[i].astype(BF16)
        xx = _mixout(xx, attn_ctx, attn_lat, yf, yb, xc, z,
                     jnp.repeat(d_skip[i], SSD_HEAD_DIM)[None], ssd_norm_g[i][None],
                     norm_g[i, 1][None], g1, wo[:DA_WIDTH], wo[DA_WIDTH:])
        wf = w_ffn_in[i].astype(BF16)
        xx = _ffn(xx, norm_g[i, 2][None], sh2, sc2, norm_g[i, 3][None], g2,
                  wf[:, :D_FF], wf[:, D_FF:], w_ffn_out[i].astype(BF16))
    return xx[:, CTX:]
```

```python
import functools
import math

import jax
import jax.numpy as jnp
from jax import lax
from jax.experimental import pallas as pl
from jax.experimental.pallas import tpu as pltpu

F32 = jnp.float32
BF16 = jnp.bfloat16

D_MODEL = 1024
N_MOD = 6
CTX = 256
GRID_W = 64
DA_HEADS = 4
DA_HEAD_DIM = 64
DA_V_DIM = 2 * DA_HEAD_DIM
DA_WIDTH = DA_HEADS * DA_V_DIM
SSD_HEADS = 8
SSD_HEAD_DIM = 64
SSD_WIDTH = SSD_HEADS * SSD_HEAD_DIM
SSD_GROUPS = 2
SSD_STATE = 128
SSD_CONV = 5
SSD_CHUNK = 128
SSD_CONV_CH = SSD_WIDTH + 2 * SSD_GROUPS * SSD_STATE
D_FF = 2816
ROPE_BASE = 10000.0
EPS = 1e-6

LANES = 128
SUBLANES = 8
MOD_TILE = 1536
ROW_TILE = 640
CONV_TILE = 256
CONV_HALO = 16
ATT_TQ = 512
ATT_TK = 1280
ATT_SLOTS = 3
ATT_TAIL = 256
ATT_RB = 16
ATT_KT = 256
ATT_PV_ROWS = 512
FFN_CHUNK = 1408
FFN_ROWS = 320
INPROJ_ROWS = 384
MIX_BLOCKS = 5
VMEM_LIMIT = 56 * 1024 * 1024
LOG2E = 1.4426950408889634
NEG_BIG = -1e30


def _cparams(n_axes):
    return pltpu.CompilerParams(
        dimension_semantics=("arbitrary",) * n_axes, vmem_limit_bytes=VMEM_LIMIT)


def _rms(u, g):
    return u * lax.rsqrt(jnp.mean(u * u, axis=-1, keepdims=True) + EPS) * g


def _silu(u):
    return u * jax.nn.sigmoid(u)


def _row_is_ctx(step, tile):
    rows = step * tile + lax.broadcasted_iota(jnp.int32, (tile, 1), 0)
    return rows < CTX


_NT = (((1,), (1,)), ((), ()))


def _const_spec(shape):
    zeros = (0,) * len(shape)
    return pl.BlockSpec(shape, lambda *_: zeros, pipeline_mode=pl.Buffered(1))


def _mod_kernel(c_ref, w_ref, b_ref, o_ref):
    s = _silu(c_ref[...])
    o_ref[...] = jnp.dot(s, w_ref[...], precision=lax.Precision.HIGHEST,
                         preferred_element_type=F32) + b_ref[...]


def _modulation(cond, w_ada, b_ada):
    depth = w_ada.shape[0]
    tn = MOD_TILE
    return pl.pallas_call(
        _mod_kernel,
        grid=(depth, N_MOD * D_MODEL // tn),
        in_specs=[pl.BlockSpec((SUBLANES, D_MODEL), lambda i, j: (0, 0)),
                  pl.BlockSpec((None, D_MODEL, tn), lambda i, j: (i, 0, j)),
                  pl.BlockSpec((None, 1, tn), lambda i, j: (i, 0, j))],
        out_specs=pl.BlockSpec((None, SUBLANES, tn), lambda i, j: (i, 0, j)),
        out_shape=jax.ShapeDtypeStruct((depth, SUBLANES, N_MOD * D_MODEL), F32),
        compiler_params=_cparams(2),
        name="adaln_mod",
    )(cond, w_ada, b_ada.reshape(depth, 1, N_MOD * D_MODEL))


def _inproj_kernel(x_ref, g_ref, sh_ref, sc_ref, cos_ref, sin_ref, dtb_ref,
                   wq_ref, wk_ref, wv_ref, wz_ref, wx_ref, wdt_ref,
                   q_ref, k_ref, v_ref, z_ref, xbc_ref, dt_ref, dtt_ref):
    tile = x_ref.shape[0]
    is_ctx_all = _row_is_ctx(pl.program_id(1), tile)
    lane = lax.broadcasted_iota(jnp.int32, (1, LANES), 1)
    first_half = (lane % DA_HEAD_DIM) < (DA_HEAD_DIM // 2)
    q_scale = DA_HEAD_DIM ** -0.5 * LOG2E

    def rope(u, rows):
        partner = jnp.where(first_half,
                            pltpu.roll(u, LANES - DA_HEAD_DIM // 2, axis=1),
                            pltpu.roll(u, DA_HEAD_DIM // 2, axis=1))
        return u * cos_ref[rows, :] + partner * sin_ref[rows, :]

    blocks = [slice(r0, min(r0 + INPROJ_ROWS, tile)) for r0 in range(0, tile, INPROJ_ROWS)]
    hs = []
    for rows in blocks:
        is_ctx = is_ctx_all[rows, :]
        shift = jnp.where(is_ctx, sh_ref[1:2, :], sh_ref[0:1, :])
        scale = jnp.where(is_ctx, sc_ref[1:2, :], sc_ref[0:1, :])
        hs.append((_rms(x_ref[rows, :], g_ref[...]) * (1.0 + scale) + shift).astype(BF16))
    for rows, h in zip(blocks, hs):
        q = jnp.dot(h, wq_ref[...], preferred_element_type=F32)
        for hd in range(DA_HEADS):
            sl = slice(hd * LANES, (hd + 1) * LANES)
            q_ref[rows, sl] = (rope(q[:, sl], rows) * q_scale).astype(BF16)
    for rows, h in zip(blocks, hs):
        k = jnp.dot(h, wk_ref[...], preferred_element_type=F32)
        for hd in range(DA_HEADS):
            sl = slice(hd * LANES, (hd + 1) * LANES)
            k_ref[rows, sl] = rope(k[:, sl], rows).astype(BF16)
    for w_ref, out_ref in ((wv_ref, v_ref), (wz_ref, z_ref), (wx_ref, xbc_ref)):
        for rows, h in zip(blocks, hs):
            out_ref[rows, :] = jnp.dot(h, w_ref[...], preferred_element_type=F32).astype(BF16)
    for rows, h in zip(blocks, hs):
        dt_raw = jnp.dot(h, wdt_ref[...], preferred_element_type=F32) + dtb_ref[...]
        dt = jnp.maximum(dt_raw, 0.0) + jnp.log1p(jnp.exp(-jnp.abs(dt_raw)))
        dt_ref[rows, :] = dt
        dtt_ref[:, rows] = dt.T[:2 * SSD_HEADS, :]


def _inproj(xx, g, sh, sc, cos_t, sin_t, dtb, wq, wk, wv, wz, wx, wdt):
    bsz, t, _ = xx.shape
    nt = t // ROW_TILE
    row = lambda w: pl.BlockSpec((None, ROW_TILE, w), lambda b, i: (b, i, 0))
    mod = pl.BlockSpec((None, 2, D_MODEL), lambda b, i: (b, 0, 0))
    tab = pl.BlockSpec((ROW_TILE, LANES), lambda b, i: (i, 0))
    return pl.pallas_call(
        _inproj_kernel,
        grid=(bsz, nt),
        in_specs=[row(D_MODEL), _const_spec((1, D_MODEL)), mod, mod, tab, tab,
                  _const_spec((1, LANES)),
                  _const_spec(wq.shape), _const_spec(wk.shape), _const_spec(wv.shape),
                  _const_spec(wz.shape), _const_spec(wx.shape), _const_spec(wdt.shape)],
        out_specs=[row(DA_WIDTH), row(DA_WIDTH), row(DA_WIDTH), row(SSD_WIDTH),
                   row(SSD_CONV_CH), row(LANES),
                   pl.BlockSpec((None, 2 * SSD_HEADS, ROW_TILE), lambda b, i: (b, 0, i))],
        out_shape=[jax.ShapeDtypeStruct((bsz, t, DA_WIDTH), BF16),
                   jax.ShapeDtypeStruct((bsz, t, DA_WIDTH), BF16),
                   jax.ShapeDtypeStruct((bsz, t, DA_WIDTH), BF16),
                   jax.ShapeDtypeStruct((bsz, t, SSD_WIDTH), BF16),
                   jax.ShapeDtypeStruct((bsz, t, SSD_CONV_CH), BF16),
                   jax.ShapeDtypeStruct((bsz, t, LANES), F32),
                   jax.ShapeDtypeStruct((bsz, 2 * SSD_HEADS, t), F32)],
        compiler_params=_cparams(2),
        name="inproj",
    )(xx, g, sh, sc, cos_t, sin_t, dtb, wq, wk, wv, wz, wx, wdt)


def _conv_kernel(cur_ref, prev_ref, next_ref, w_ref, b_ref, o_ref):
    i = pl.program_id(1)
    n = pl.num_programs(1)
    slabs = cur_ref.shape[0]
    first_of_segment = jnp.logical_or(i == 0, i == CTX // CONV_TILE)
    last_of_segment = jnp.logical_or(i == CTX // CONV_TILE - 1, i == n - 1)
    cur = cur_ref[...].astype(F32)
    prev = jnp.where(first_of_segment, 0.0, prev_ref[prev_ref.shape[0] - 1].astype(F32))
    nxt = jnp.where(last_of_segment, 0.0, next_ref[0].astype(F32))
    xe = jnp.concatenate([prev[None, :SUBLANES], cur[:, :SUBLANES], nxt[None, :SUBLANES]], axis=0)
    xo = jnp.concatenate([prev[None, SUBLANES:], cur[:, SUBLANES:], nxt[None, SUBLANES:]], axis=0)
    w = [w_ref[k] for k in range(SSD_CONV)]
    bias = b_ref[...]
    lo, mid, hi = slice(0, slabs), slice(1, slabs + 1), slice(2, slabs + 2)
    ye = bias + w[0] * xe[lo] + w[1] * xo[lo] + w[2] * xe[mid] + w[3] * xo[mid] + w[4] * xe[hi]
    yo = bias + w[0] * xo[lo] + w[1] * xe[mid] + w[2] * xo[mid] + w[3] * xe[hi] + w[4] * xo[hi]
    o_ref[...] = jnp.concatenate([_silu(ye), _silu(yo)], axis=1).astype(BF16)


def _conv(xbc, w, b):
    bsz, t, ch = xbc.shape
    assert ch == SUBLANES * LANES and SSD_CONV == 5
    nt = t // CONV_TILE
    slabs, halo = CONV_TILE // 2, CONV_HALO // 2
    per = slabs // halo
    last_halo = t // 2 // halo - 1
    tile = (2 * SUBLANES, LANES)
    paired = xbc.reshape(bsz, t // 2, *tile)
    out = pl.pallas_call(
        _conv_kernel,
        grid=(bsz, nt),
        in_specs=[pl.BlockSpec((None, slabs) + tile, lambda b_, i: (b_, i, 0, 0)),
                  pl.BlockSpec((None, halo) + tile,
                               lambda b_, i: (b_, jnp.maximum(i * per - 1, 0), 0, 0)),
                  pl.BlockSpec((None, halo) + tile,
                               lambda b_, i: (b_, jnp.minimum((i + 1) * per, last_halo), 0, 0)),
                  _const_spec((SSD_CONV, SUBLANES, LANES)), _const_spec((SUBLANES, LANES))],
        out_specs=pl.BlockSpec((None, slabs) + tile, lambda b_, i: (b_, i, 0, 0)),
        out_shape=jax.ShapeDtypeStruct(paired.shape, BF16),
        compiler_params=_cparams(2),
        name="dwconv",
    )(paired, paired, paired, w.reshape(SSD_CONV, SUBLANES, LANES), b.reshape(SUBLANES, LANES))
    return out.reshape(bsz, t, ch)


def _ssd_chains(chains, arow_ref, acol_ref):
    q = SSD_CHUNK
    hi = lax.Precision.HIGHEST
    ii = lax.broadcasted_iota(jnp.int32, (q, q), 0)
    jj = lax.broadcasted_iota(jnp.int32, (q, q), 1)
    lane = lax.broadcasted_iota(jnp.int32, (1, LANES), 1)
    low_half = lane < SSD_HEAD_DIM
    a_row = -jnp.exp(arow_ref[...])
    a_col = -jnp.exp(acol_ref[...])
    heads_per_group = SSD_HEADS // SSD_GROUPS
    st = []
    for d, xc_ref, dt_ref, dtt_ref, _, _ in chains:
        dtt = dtt_ref[...]
        dat = dtt * a_col
        mask = (jj <= ii) if d == 0 else (jj >= ii)
        mf = mask.astype(F32)
        st.append(dict(
            xall=xc_ref[...], dtt=dtt, mask=mask,
            cum_col=jnp.dot(mf, dt_ref[...] * a_row, precision=hi,
                            preferred_element_type=F32),
            cum_row=lax.dot_general(dat, mf, _NT, precision=hi,
                                    preferred_element_type=F32),
            tot=jnp.sum(dat, axis=1, keepdims=True)))
    for g in range(SSD_GROUPS):
        for c in st:
            bg = c["xall"][:, SSD_WIDTH + g * SSD_STATE:SSD_WIDTH + (g + 1) * SSD_STATE]
            c0 = SSD_WIDTH + SSD_GROUPS * SSD_STATE + g * SSD_STATE
            cg = c["xall"][:, c0:c0 + SSD_STATE]
            c["cb"] = lax.dot_general(cg, bg, _NT, preferred_element_type=F32)
            c["bgt"] = bg.astype(F32).T
            c["cgf"] = cg.astype(F32)
        for pr in range(heads_per_group // 2):
            pair = g * (heads_per_group // 2) + pr
            for c, (_, _, _, _, s_ref, _) in zip(st, chains):
                c["xp"] = c["xall"][:, pair * LANES:(pair + 1) * LANES]
                c["sp"] = s_ref[pair]
                c["spb"] = c["sp"].astype(BF16)
                c["ys"], c["us"], c["decs"] = [], [], []
            for r in range(2):
                for c, (d, *_) in zip(st, chains):
                    ln = SSD_HEADS * d + 2 * pair + r
                    colb = c["cum_col"][:, ln:ln + 1]
                    rowb = c["cum_row"][ln:ln + 1, :]
                    dtr = c["dtt"][ln:ln + 1, :]
                    tot_h = c["tot"][ln:ln + 1, :]
                    lmat = jnp.exp(jnp.where(c["mask"], colb - rowb, -jnp.inf))
                    w = (c["cb"] * lmat * dtr).astype(BF16)
                    e = (c["cgf"] * jnp.exp(colb)).astype(BF16)
                    c["ys"].append(jnp.dot(w, c["xp"], preferred_element_type=F32)
                                   + jnp.dot(e, c["spb"], preferred_element_type=F32))
                    wrow = dtr * jnp.exp(tot_h - rowb)
                    c["us"].append(jnp.dot((c["bgt"] * wrow).astype(BF16), c["xp"],
                                           preferred_element_type=F32))
                    c["decs"].append(jnp.exp(tot_h))
            for c, (_, _, _, _, s_ref, y_ref) in zip(st, chains):
                y_ref[:, pair * LANES:(pair + 1) * LANES] = jnp.where(
                    low_half, c["ys"][0], c["ys"][1])
                dec = jnp.where(low_half, c["decs"][0], c["decs"][1])
                s_ref[pair] = c["sp"] * dec + jnp.where(low_half, c["us"][0], c["us"][1])


def _ssd_kernel(xf_ref, dtf_ref, dttf_ref, xb_ref, dtb_ref, dttb_ref, arow_ref, acol_ref,
                yf_ref, yb_ref, s_ref):
    @pl.when(pl.program_id(0) == 0)
    def _():
        s_ref[...] = jnp.zeros_like(s_ref)

    chains = []
    for b in range(xf_ref.shape[0]):
        chains.append((0, xf_ref.at[b], dtf_ref.at[b], dttf_ref.at[b], s_ref.at[b, 0],
                       yf_ref.at[b]))
        chains.append((1, xb_ref.at[b], dtb_ref.at[b], dttb_ref.at[b], s_ref.at[b, 1],
                       yb_ref.at[b]))
    _ssd_chains(chains, arow_ref, acol_ref)


def _ssd(xc, dt, dtt, arow, acol):
    bsz, t, ch = xc.shape
    nc = t // SSD_CHUNK
    nctx = CTX // SSD_CHUNK

    def fwd(c):
        return c

    def bwd(c):
        return jnp.where(c < nctx, nctx - 1 - c, nc - 1 + nctx - c)

    def specs(order):
        return [pl.BlockSpec((bsz, SSD_CHUNK, ch), lambda c: (0, order(c), 0)),
                pl.BlockSpec((bsz, SSD_CHUNK, LANES), lambda c: (0, order(c), 0)),
                pl.BlockSpec((bsz, 2 * SSD_HEADS, SSD_CHUNK), lambda c: (0, 0, order(c)))]

    y_spec = lambda order: pl.BlockSpec((bsz, SSD_CHUNK, SSD_WIDTH),
                                        lambda c: (0, order(c), 0))
    y_shape = jax.ShapeDtypeStruct((bsz, t, SSD_WIDTH), F32)
    return pl.pallas_call(
        _ssd_kernel,
        grid=(nc,),
        in_specs=specs(fwd) + specs(bwd) + [_const_spec((1, LANES)),
                                            _const_spec((2 * SSD_HEADS, 1))],
        out_specs=[y_spec(fwd), y_spec(bwd)],
        out_shape=[y_shape, y_shape],
        scratch_shapes=[pltpu.VMEM((bsz, 2, SSD_HEADS // 2, SSD_STATE, LANES), F32)],
        compiler_params=_cparams(1),
        name="ssd_scan",
    )(xc, dt, dtt, xc, dt, dtt, arow, acol)


def _masked_q(q):
    lane = lax.broadcasted_iota(jnp.int32, (1, LANES), 1)
    comp0 = lane < DA_HEAD_DIM
    zero = jnp.zeros_like(q)
    return jnp.where(comp0, q, zero), jnp.where(comp0, zero, q)


def _diff_out(lam_init, acc0, acc1, lv_ref, sg_ref):
    lv = lv_ref[...]
    lam = (jnp.exp(jnp.sum(lv[0:1] * lv[1:2], axis=1, keepdims=True))
           - jnp.exp(jnp.sum(lv[2:3] * lv[3:4], axis=1, keepdims=True)) + lam_init)
    o = acc0[:, :LANES] / acc0[:, LANES:] - lam * (acc1[:, :LANES] / acc1[:, LANES:])
    return (_rms(o, sg_ref[...]) * (1.0 - lam_init)).astype(BF16)


def _attn_ctx_kernel(lam_init, q_ref, k_ref, v_ref, lv_ref, sg_ref, o_ref):
    kc = k_ref[...]
    vext = jnp.concatenate([v_ref[...], jnp.ones((CTX, LANES), BF16)], axis=1)
    accs = []
    for qm in _masked_q(q_ref[...]):
        s = lax.dot_general(qm, kc, _NT, preferred_element_type=F32)
        p = jnp.exp2(s - jnp.max(s, axis=1, keepdims=True)).astype(BF16)
        accs.append(jnp.dot(p, vext, preferred_element_type=F32))
    o_ref[...] = _diff_out(lam_init, accs[0], accs[1], lv_ref, sg_ref)


def _attn_lat_kernel(lam_init, qa_ref, qb_ref, k_ref, v_ref, lv_ref, sg_ref, o_ref,
                     vext_ref, acc_ref, m_ref, qm_ref, p_ref, al_ref, *s_bufs):
    t = k_ref.shape[0]
    assert t % ATT_TK == 0 and ATT_TAIL % ATT_KT == 0 and len(s_bufs) == ATT_SLOTS

    @pl.when(pl.program_id(2) == 0)
    def _():
        vext_ref[:, :LANES] = v_ref[...]
        vext_ref[:, LANES:] = jnp.ones((t, LANES), BF16)

    half = ATT_TQ // 2
    for c, (qa, qb) in enumerate(zip(_masked_q(qa_ref[...]), _masked_q(qb_ref[...]))):
        qm_ref[c * ATT_TQ:c * ATT_TQ + half, :] = qa
        qm_ref[c * ATT_TQ + half:(c + 1) * ATT_TQ, :] = qb
    acc_ref[...] = jnp.zeros_like(acc_ref)
    m_ref[...] = jnp.full(m_ref.shape, NEG_BIG, F32)

    bounds = list(range(0, t, ATT_TK)) + [t - ATT_TAIL, t]
    n = len(bounds) - 1
    rows_all = 2 * ATT_TQ
    nseg = rows_all // ATT_PV_ROWS

    def key_tiles(j):
        return (bounds[j + 1] - bounds[j]) // ATT_KT if j < n else 0

    def scores_piece(j, kt0, kt1):
        lo, hi = kt0 * ATT_KT, kt1 * ATT_KT
        s_bufs[j % ATT_SLOTS][:, lo:hi] = lax.dot_general(
            qm_ref[...], k_ref[bounds[j] + lo:bounds[j] + hi, :], _NT,
            preferred_element_type=F32)

    def softmax_piece(j, r):
        rows = slice(r, r + ATT_RB)
        size = bounds[j + 1] - bounds[j]
        s = s_bufs[j % ATT_SLOTS][rows, :size]
        m_prev = m_ref[rows, :]
        m_next = jnp.maximum(m_prev, jnp.max(s, axis=1, keepdims=True))
        al_ref[rows, :] = jnp.exp2(m_prev - m_next)
        m_ref[rows, :] = m_next
        m_wide = jnp.concatenate([m_next] * (size // LANES), axis=1)
        p_ref[rows, :size] = jnp.exp2(s - m_wide).astype(BF16)

    def weighted_piece(j, seg):
        rows = slice(seg * ATT_PV_ROWS, (seg + 1) * ATT_PV_ROWS)
        size = bounds[j + 1] - bounds[j]
        pv = jnp.dot(p_ref[rows, :size], vext_ref[bounds[j]:bounds[j + 1], :],
                     preferred_element_type=F32)
        al = al_ref[rows, :]
        acc_ref[rows, :] = acc_ref[rows, :] * jnp.concatenate([al, al], axis=1) + pv

    scores_piece(0, 0, key_tiles(0))
    for j in range(n):
        tiles_next = key_tiles(j + 1)
        for seg in range(nseg):
            for r in range(seg * ATT_PV_ROWS, (seg + 1) * ATT_PV_ROWS, ATT_RB):
                softmax_piece(j, r)
            kt0, kt1 = seg * tiles_next // nseg, (seg + 1) * tiles_next // nseg
            if kt1 > kt0:
                scores_piece(j + 1, kt0, kt1)
            weighted_piece(j, seg)
    o_ref[...] = _diff_out(lam_init, acc_ref[:ATT_TQ, :], acc_ref[ATT_TQ:, :], lv_ref, sg_ref)


def _attention(q, k, v, lam_vec, sub_g, lam_init):
    bsz, t, _ = q.shape
    seq = t - CTX
    part = functools.partial
    consts = [_const_spec((4, DA_HEAD_DIM)), _const_spec((1, DA_V_DIM))]
    ctx_blk = pl.BlockSpec((None, CTX, LANES), lambda b, h: (b, 0, h))
    o_ctx = pl.pallas_call(
        part(_attn_ctx_kernel, lam_init),
        grid=(bsz, DA_HEADS),
        in_specs=[ctx_blk, ctx_blk, ctx_blk] + consts,
        out_specs=ctx_blk,
        out_shape=jax.ShapeDtypeStruct((bsz, CTX, DA_WIDTH), BF16),
        compiler_params=_cparams(2),
        name="diff_attn_ctx",
    )(q, k, v, lam_vec, sub_g)

    assert ATT_TQ == 2 * CTX
    kv = pl.BlockSpec((None, t, LANES), lambda b, h, i: (b, 0, h),
                      pipeline_mode=pl.Buffered(1))
    qa = pl.BlockSpec((None, CTX, LANES), lambda b, h, i: (b, 2 * i + 1, h))
    qb = pl.BlockSpec((None, CTX, LANES), lambda b, h, i: (b, 2 * i + 2, h))
    s_buf = pltpu.VMEM((2 * ATT_TQ, ATT_TK), F32)
    p_buf = pltpu.VMEM((2 * ATT_TQ, ATT_TK), BF16)
    al_buf = pltpu.VMEM((2 * ATT_TQ, LANES), F32)
    o_lat = pl.pallas_call(
        part(_attn_lat_kernel, lam_init),
        grid=(bsz, DA_HEADS, seq // ATT_TQ),
        in_specs=[qa, qb, kv, kv] + consts,
        out_specs=pl.BlockSpec((None, ATT_TQ, LANES), lambda b, h, i: (b, i, h)),
        out_shape=jax.ShapeDtypeStruct((bsz, seq, DA_WIDTH), BF16),
        scratch_shapes=[pltpu.VMEM((t, 2 * LANES), BF16),
                        pltpu.VMEM((2 * ATT_TQ, 2 * LANES), F32),
                        pltpu.VMEM((2 * ATT_TQ, LANES), F32),
                        pltpu.VMEM((2 * ATT_TQ, LANES), BF16), p_buf, al_buf]
                       + [s_buf] * ATT_SLOTS,
        compiler_params=_cparams(3),
        name="diff_attn",
    )(q, q, k, v, lam_vec, sub_g)
    return o_ctx, o_lat


def _mixout_kernel(x_ref, actx_ref, *refs):
    alat_refs = refs[:MIX_BLOCKS]
    (yf_ref, yb_ref, xc_ref, z_ref, dsk_ref, sg_ref, g_ref, gate_ref, wa_ref, ws_ref,
     o_ref) = refs[MIX_BLOCKS:]
    first_step = pl.program_id(1) == 0
    for k in range(MIX_BLOCKS):
        rows = slice(k * CTX, (k + 1) * CTX)
        attn = alat_refs[k][...]
        gate = gate_ref[0:1, :]
        if k == 0:
            attn = jnp.where(first_step, actx_ref[...], attn)
            gate = jnp.where(first_step, gate_ref[1:2, :], gate)
        y = yf_ref[rows, :] + yb_ref[rows, :] + dsk_ref[...] * xc_ref[rows, :].astype(F32)
        u = y * _silu(z_ref[rows, :].astype(F32))
        ssd = _rms(u, sg_ref[...]).astype(BF16)
        mix = (jnp.dot(attn, wa_ref[...], preferred_element_type=F32)
               + jnp.dot(ssd, ws_ref[...], preferred_element_type=F32))
        o_ref[rows, :] = x_ref[rows, :] + gate * _rms(mix, g_ref[...])


def _mixout(xx, attn_ctx, attn_lat, yf, yb, xc, z, dsk, ssd_g, g, gate, wa, ws):
    bsz, t, _ = xx.shape
    tile = MIX_BLOCKS * CTX
    assert t % tile == 0
    row = lambda w: pl.BlockSpec((None, tile, w), lambda b, i: (b, i, 0))
    mod = pl.BlockSpec((None, 2, D_MODEL), lambda b, i: (b, 0, 0))

    def lat_block(k):
        return pl.BlockSpec((None, CTX, DA_WIDTH),
                            lambda b, i: (b, jnp.maximum(MIX_BLOCKS * i + k - 1, 0), 0))

    return pl.pallas_call(
        _mixout_kernel,
        grid=(bsz, t // tile),
        in_specs=[row(D_MODEL), pl.BlockSpec((None, CTX, DA_WIDTH), lambda b, i: (b, 0, 0))]
                 + [lat_block(k) for k in range(MIX_BLOCKS)]
                 + [row(SSD_WIDTH), row(SSD_WIDTH),
                    row(SSD_WIDTH),
                    row(SSD_WIDTH), _const_spec((1, SSD_WIDTH)), _const_spec((1, SSD_WIDTH)),
                    _const_spec((1, D_MODEL)), mod, _const_spec(wa.shape),
                    _const_spec(ws.shape)],
        out_specs=row(D_MODEL),
        out_shape=jax.ShapeDtypeStruct(xx.shape, F32),
        compiler_params=_cparams(2),
        name="mix_out",
    )(xx, attn_ctx, *([attn_lat] * MIX_BLOCKS), yf, yb, xc, z, dsk, ssd_g, g, gate, wa, ws)


def _ffn_kernel(x_ref, gin_ref, sh_ref, sc_ref, gout_ref, gate_ref, wg_ref, wu_ref, wo_ref,
                o_ref):
    tile = x_ref.shape[0]
    is_ctx_all = _row_is_ctx(pl.program_id(1), tile)
    blocks = [slice(r0, r0 + FFN_ROWS) for r0 in range(0, tile, FFN_ROWS)]
    pick = lambda r, rows: jnp.where(is_ctx_all[rows, :], r[1:2, :], r[0:1, :])
    hs = [(_rms(x_ref[rows, :], gin_ref[...]) * (1.0 + pick(sc_ref, rows))
           + pick(sh_ref, rows)).astype(BF16) for rows in blocks]
    accs = [None] * len(blocks)
    for c in range(D_FF // FFN_CHUNK):
        sl = slice(c * FFN_CHUNK, (c + 1) * FFN_CHUNK)
        for b, h in enumerate(hs):
            gt = jnp.dot(h, wg_ref[:, sl], preferred_element_type=F32)
            up = jnp.dot(h, wu_ref[:, sl], preferred_element_type=F32)
            a = (_silu(gt) * up).astype(BF16)
            part = jnp.dot(a, wo_ref[sl, :], preferred_element_type=F32)
            accs[b] = part if accs[b] is None else accs[b] + part
    for rows, acc in zip(blocks, accs):
        o_ref[rows, :] = x_ref[rows, :] + pick(gate_ref, rows) * _rms(acc, gout_ref[...])


def _ffn(xx, gin, sh, sc, gout, gate, wg, wu, wo):
    bsz, t, _ = xx.shape
    nt = t // ROW_TILE
    row = pl.BlockSpec((None, ROW_TILE, D_MODEL), lambda b, i: (b, i, 0))
    mod = pl.BlockSpec((None, 2, D_MODEL), lambda b, i: (b, 0, 0))
    return pl.pallas_call(
        _ffn_kernel,
        grid=(bsz, nt),
        in_specs=[row, _const_spec((1, D_MODEL)), mod, mod, _const_spec((1, D_MODEL)), mod,
                  _const_spec(wg.shape), _const_spec(wu.shape), _const_spec(wo.shape)],
        out_specs=row,
        out_shape=jax.ShapeDtypeStruct(xx.shape, F32),
        compiler_params=_cparams(2),
        name="ffn",
    )(xx, gin, sh, sc, gout, gate, wg, wu, wo)


def _rope_tables(seq):
    rows = seq // GRID_W
    row = jnp.repeat(jnp.arange(rows, dtype=F32), GRID_W)
    col = jnp.tile(jnp.arange(GRID_W, dtype=F32), rows)
    nf = DA_HEAD_DIM // 4
    inv = ROPE_BASE ** (-jnp.arange(nf, dtype=F32) / nf)
    ang = jnp.concatenate([row[:, None] * inv, col[:, None] * inv], axis=-1)
    cos = jnp.concatenate([jnp.ones((CTX, DA_HEAD_DIM // 2), F32), jnp.cos(ang)], axis=0)
    sin = jnp.concatenate([jnp.zeros((CTX, DA_HEAD_DIM // 2), F32), jnp.sin(ang)], axis=0)
    return jnp.tile(cos, (1, 4)), jnp.tile(jnp.concatenate([-sin, sin], axis=1), (1, 2))


def kernel(x, c, ctx, c_ctx, w_ada, b_ada, norm_g, w_in, conv_w, conv_b, a_log, dt_bias,
           d_skip, ssd_norm_g, diff_lambda, subln_g, w_out, w_ffn_in, w_ffn_out):
    bsz, seq, _ = x.shape
    depth = w_ada.shape[0]
    t = CTX + seq
    assert ctx.shape[1] == CTX and t % ROW_TILE == 0 and seq % ATT_TQ == 0

    xx = jnp.concatenate([ctx, x], axis=1)
    cond = jnp.zeros((SUBLANES, D_MODEL), F32).at[:bsz].set(c).at[bsz].set(c_ctx)
    mod = _modulation(cond, w_ada, b_ada)
    cos_t, sin_t = _rope_tables(seq)

    q_end = DA_WIDTH
    k_end = 2 * DA_WIDTH
    v_end = 3 * DA_WIDTH
    z_end = v_end + SSD_WIDTH
    xbc_end = z_end + SSD_CONV_CH

    for i in range(depth):
        lam_init = 0.8 - 0.6 * math.exp(-0.3 * i)
        mods = [jnp.stack([mod[i, :bsz, j * D_MODEL:(j + 1) * D_MODEL],
                           jnp.broadcast_to(mod[i, bsz, j * D_MODEL:(j + 1) * D_MODEL],
                                            (bsz, D_MODEL))], axis=1) for j in range(N_MOD)]
        sh1, sc1, g1, sh2, sc2, g2 = mods
        wi = w_in[i].astype(BF16)
        wdt = jnp.zeros((D_MODEL, LANES), BF16).at[:, :2 * SSD_HEADS].set(wi[:, xbc_end:])
        lane16 = lambda a: jnp.zeros((1, LANES), F32).at[0, :2 * SSD_HEADS].set(a.reshape(-1))
        q, k, v, z, xbc, dt, dtt = _inproj(
            xx, norm_g[i, 0][None], sh1, sc1, cos_t, sin_t, lane16(dt_bias[i]),
            wi[:, :q_end], wi[:, q_end:k_end], wi[:, k_end:v_end], wi[:, v_end:z_end],
            wi[:, z_end:xbc_end], wdt)
        xc = _conv(xbc, conv_w[i], conv_b[i])
        yf, yb = _ssd(xc, dt, dtt, lane16(a_log[i]), a_log[i].reshape(2 * SSD_HEADS, 1))
        attn_ctx, attn_lat = _attention(q, k, v, diff_lambda[i], subln_g[i][None], lam_init)
        wo = w_out[i].astype(BF16)
        xx = _mixout(xx, attn_ctx, attn_lat, yf, yb, xc, z,
                     jnp.repeat(d_skip[i], SSD_HEAD_DIM)[None], ssd_norm_g[i][None],
                     norm_g[i, 1][None], g1, wo[:DA_WIDTH], wo[DA_WIDTH:])
        wf = w_ffn_in[i].astype(BF16)
        xx = _ffn(xx, norm_g[i, 2][None], sh2, sc2, norm_g[i, 3][None], g2,
                  wf[:, :D_FF], wf[:, D_FF:], w_ffn_out[i].astype(BF16))
    return xx[:, CTX:]
```

```python
import functools
import math

import jax
import jax.numpy as jnp
from jax import lax
from jax.experimental import pallas as pl
from jax.experimental.pallas import tpu as pltpu

F32 = jnp.float32
BF16 = jnp.bfloat16

D_MODEL = 1024
N_MOD = 6
CTX = 256
GRID_W = 64
DA_HEADS = 4
DA_HEAD_DIM = 64
DA_V_DIM = 2 * DA_HEAD_DIM
DA_WIDTH = DA_HEADS * DA_V_DIM
SSD_HEADS = 8
SSD_HEAD_DIM = 64
SSD_WIDTH = SSD_HEADS * SSD_HEAD_DIM
SSD_GROUPS = 2
SSD_STATE = 128
SSD_CONV = 5
SSD_CHUNK = 128
SSD_CONV_CH = SSD_WIDTH + 2 * SSD_GROUPS * SSD_STATE
D_FF = 2816
ROPE_BASE = 10000.0
EPS = 1e-6

LANES = 128
SUBLANES = 8
MOD_TILE = 1536
ROW_TILE = 640
CONV_TILE = 256
CONV_HALO = 16
ATT_TQ = 512
ATT_TK = 1280
ATT_SLOTS = 3
ATT_TAIL = 256
ATT_RB = 16
ATT_KT = 256
ATT_PV_ROWS = 1024
FFN_CHUNK = 1408
FFN_ROWS = 320
INPROJ_ROWS = 384
MIX_BLOCKS = 5
VMEM_LIMIT = 56 * 1024 * 1024
LOG2E = 1.4426950408889634
NEG_BIG = -1e30


def _cparams(n_axes):
    return pltpu.CompilerParams(
        dimension_semantics=("arbitrary",) * n_axes, vmem_limit_bytes=VMEM_LIMIT)


def _rms(u, g):
    return u * lax.rsqrt(jnp.mean(u * u, axis=-1, keepdims=True) + EPS) * g


def _silu(u):
    return u * jax.nn.sigmoid(u)


def _row_is_ctx(step, tile):
    rows = step * tile + lax.broadcasted_iota(jnp.int32, (tile, 1), 0)
    return rows < CTX


_NT = (((1,), (1,)), ((), ()))


def _const_spec(shape):
    zeros = (0,) * len(shape)
    return pl.BlockSpec(shape, lambda *_: zeros, pipeline_mode=pl.Buffered(1))


def _mod_kernel(c_ref, w_ref, b_ref, o_ref):
    s = _silu(c_ref[...])
    o_ref[...] = jnp.dot(s, w_ref[...], precision=lax.Precision.HIGHEST,
                         preferred_element_type=F32) + b_ref[...]


def _modulation(cond, w_ada, b_ada):
    depth = w_ada.shape[0]
    tn = MOD_TILE
    return pl.pallas_call(
        _mod_kernel,
        grid=(depth, N_MOD * D_MODEL // tn),
        in_specs=[pl.BlockSpec((SUBLANES, D_MODEL), lambda i, j: (0, 0)),
                  pl.BlockSpec((None, D_MODEL, tn), lambda i, j: (i, 0, j)),
                  pl.BlockSpec((None, 1, tn), lambda i, j: (i, 0, j))],
        out_specs=pl.BlockSpec((None, SUBLANES, tn), lambda i, j: (i, 0, j)),
        out_shape=jax.ShapeDtypeStruct((depth, SUBLANES, N_MOD * D_MODEL), F32),
        compiler_params=_cparams(2),
        name="adaln_mod",
    )(cond, w_ada, b_ada.reshape(depth, 1, N_MOD * D_MODEL))


def _inproj_kernel(x_ref, g_ref, sh_ref, sc_ref, cos_ref, sin_ref, dtb_ref,
                   wq_ref, wk_ref, wv_ref, wz_ref, wx_ref, wdt_ref,
                   q_ref, k_ref, v_ref, z_ref, xbc_ref, dt_ref, dtt_ref):
    tile = x_ref.shape[0]
    is_ctx_all = _row_is_ctx(pl.program_id(1), tile)
    lane = lax.broadcasted_iota(jnp.int32, (1, LANES), 1)
    first_half = (lane % DA_HEAD_DIM) < (DA_HEAD_DIM // 2)
    q_scale = DA_HEAD_DIM ** -0.5 * LOG2E

    def rope(u, rows):
        partner = jnp.where(first_half,
                            pltpu.roll(u, LANES - DA_HEAD_DIM // 2, axis=1),
                            pltpu.roll(u, DA_HEAD_DIM // 2, axis=1))
        return u * cos_ref[rows, :] + partner * sin_ref[rows, :]

    blocks = [slice(r0, min(r0 + INPROJ_ROWS, tile)) for r0 in range(0, tile, INPROJ_ROWS)]
    hs = []
    for rows in blocks:
        is_ctx = is_ctx_all[rows, :]
        shift = jnp.where(is_ctx, sh_ref[1:2, :], sh_ref[0:1, :])
        scale = jnp.where(is_ctx, sc_ref[1:2, :], sc_ref[0:1, :])
        hs.append((_rms(x_ref[rows, :], g_ref[...]) * (1.0 + scale) + shift).astype(BF16))
    for rows, h in zip(blocks, hs):
        q = jnp.dot(h, wq_ref[...], preferred_element_type=F32)
        for hd in range(DA_HEADS):
            sl = slice(hd * LANES, (hd + 1) * LANES)
            q_ref[rows, sl] = (rope(q[:, sl], rows) * q_scale).astype(BF16)
    for rows, h in zip(blocks, hs):
        k = jnp.dot(h, wk_ref[...], preferred_element_type=F32)
        for hd in range(DA_HEADS):
            sl = slice(hd * LANES, (hd + 1) * LANES)
            k_ref[rows, sl] = rope(k[:, sl], rows).astype(BF16)
    for w_ref, out_ref in ((wv_ref, v_ref), (wz_ref, z_ref), (wx_ref, xbc_ref)):
        for rows, h in zip(blocks, hs):
            out_ref[rows, :] = jnp.dot(h, w_ref[...], preferred_element_type=F32).astype(BF16)
    for rows, h in zip(blocks, hs):
        dt_raw = jnp.dot(h, wdt_ref[...], preferred_element_type=F32) + dtb_ref[...]
        dt = jnp.maximum(dt_raw, 0.0) + jnp.log1p(jnp.exp(-jnp.abs(dt_raw)))
        dt_ref[rows, :] = dt
        dtt_ref[:, rows] = dt.T[:2 * SSD_HEADS, :]


def _inproj(xx, g, sh, sc, cos_t, sin_t, dtb, wq, wk, wv, wz, wx, wdt):
    bsz, t, _ = xx.shape
    nt = t // ROW_TILE
    row = lambda w: pl.BlockSpec((None, ROW_TILE, w), lambda b, i: (b, i, 0))
    mod = pl.BlockSpec((None, 2, D_MODEL), lambda b, i: (b, 0, 0))
    tab = pl.BlockSpec((ROW_TILE, LANES), lambda b, i: (i, 0))
    return pl.pallas_call(
        _inproj_kernel,
        grid=(bsz, nt),
        in_specs=[row(D_MODEL), _const_spec((1, D_MODEL)), mod, mod, tab, tab,
                  _const_spec((1, LANES)),
                  _const_spec(wq.shape), _const_spec(wk.shape), _const_spec(wv.shape),
                  _const_spec(wz.shape), _const_spec(wx.shape), _const_spec(wdt.shape)],
        out_specs=[row(DA_WIDTH), row(DA_WIDTH), row(DA_WIDTH), row(SSD_WIDTH),
                   row(SSD_CONV_CH), row(LANES),
                   pl.BlockSpec((None, 2 * SSD_HEADS, ROW_TILE), lambda b, i: (b, 0, i))],
        out_shape=[jax.ShapeDtypeStruct((bsz, t, DA_WIDTH), BF16),
                   jax.ShapeDtypeStruct((bsz, t, DA_WIDTH), BF16),
                   jax.ShapeDtypeStruct((bsz, t, DA_WIDTH), BF16),
                   jax.ShapeDtypeStruct((bsz, t, SSD_WIDTH), BF16),
                   jax.ShapeDtypeStruct((bsz, t, SSD_CONV_CH), BF16),
                   jax.ShapeDtypeStruct((bsz, t, LANES), F32),
                   jax.ShapeDtypeStruct((bsz, 2 * SSD_HEADS, t), F32)],
        compiler_params=_cparams(2),
        name="inproj",
    )(xx, g, sh, sc, cos_t, sin_t, dtb, wq, wk, wv, wz, wx, wdt)


def _conv_kernel(cur_ref, prev_ref, next_ref, w_ref, b_ref, o_ref, ext_ref):
    i = pl.program_id(1)
    n = pl.num_programs(1)
    half = SSD_CONV // 2
    pad = SUBLANES
    first_of_segment = jnp.logical_or(i == 0, i == CTX // CONV_TILE)
    last_of_segment = jnp.logical_or(i == CTX // CONV_TILE - 1, i == n - 1)
    prev = prev_ref[...].astype(F32)[CONV_HALO - pad:, :]
    nxt = next_ref[...].astype(F32)[:pad, :]
    ext_ref[0:pad, :] = jnp.where(first_of_segment, 0.0, prev)
    ext_ref[pad:pad + CONV_TILE, :] = cur_ref[...].astype(F32)
    ext_ref[pad + CONV_TILE:, :] = jnp.where(last_of_segment, 0.0, nxt)
    acc = b_ref[...] + w_ref[half:half + 1, :] * ext_ref[pad:pad + CONV_TILE, :]
    for s in range(-half, half + 1):
        if s != 0:
            acc = acc + w_ref[half + s:half + s + 1, :] * ext_ref[pl.ds(pad + s, CONV_TILE), :]
    o_ref[...] = _silu(acc).astype(BF16)


def _conv(xbc, w, b):
    bsz, t, ch = xbc.shape
    nt = t // CONV_TILE
    per = CONV_TILE // CONV_HALO
    last_halo = t // CONV_HALO - 1
    return pl.pallas_call(
        _conv_kernel,
        grid=(bsz, nt),
        in_specs=[pl.BlockSpec((None, CONV_TILE, ch), lambda b_, i: (b_, i, 0)),
                  pl.BlockSpec((None, CONV_HALO, ch),
                               lambda b_, i: (b_, jnp.maximum(i * per - 1, 0), 0)),
                  pl.BlockSpec((None, CONV_HALO, ch),
                               lambda b_, i: (b_, jnp.minimum((i + 1) * per, last_halo), 0)),
                  _const_spec((SUBLANES, ch)), _const_spec((1, ch))],
        out_specs=pl.BlockSpec((None, CONV_TILE, ch), lambda b_, i: (b_, i, 0)),
        out_shape=jax.ShapeDtypeStruct((bsz, t, ch), BF16),
        scratch_shapes=[pltpu.VMEM((CONV_TILE + 2 * SUBLANES, ch), F32)],
        compiler_params=_cparams(2),
        name="dwconv",
    )(xbc, xbc, xbc, w, b)


def _ssd_chains(chains, arow_ref, acol_ref):
    q = SSD_CHUNK
    hi = lax.Precision.HIGHEST
    ii = lax.broadcasted_iota(jnp.int32, (q, q), 0)
    jj = lax.broadcasted_iota(jnp.int32, (q, q), 1)
    lane = lax.broadcasted_iota(jnp.int32, (1, LANES), 1)
    low_half = lane < SSD_HEAD_DIM
    a_row = -jnp.exp(arow_ref[...])
    a_col = -jnp.exp(acol_ref[...])
    heads_per_group = SSD_HEADS // SSD_GROUPS
    st = []
    for d, xc_ref, dt_ref, dtt_ref, _, _ in chains:
        dtt = dtt_ref[...]
        dat = dtt * a_col
        mask = (jj <= ii) if d == 0 else (jj >= ii)
        mf = mask.astype(F32)
        st.append(dict(
            xall=xc_ref[...], dtt=dtt, mask=mask,
            cum_col=jnp.dot(mf, dt_ref[...] * a_row, precision=hi,
                            preferred_element_type=F32),
            cum_row=lax.dot_general(dat, mf, _NT, precision=hi,
                                    preferred_element_type=F32),
            tot=jnp.sum(dat, axis=1, keepdims=True)))
    for g in range(SSD_GROUPS):
        for c in st:
            bg = c["xall"][:, SSD_WIDTH + g * SSD_STATE:SSD_WIDTH + (g + 1) * SSD_STATE]
            c0 = SSD_WIDTH + SSD_GROUPS * SSD_STATE + g * SSD_STATE
            cg = c["xall"][:, c0:c0 + SSD_STATE]
            c["cb"] = lax.dot_general(cg, bg, _NT, preferred_element_type=F32)
            c["bgt"] = bg.astype(F32).T
            c["cgf"] = cg.astype(F32)
        for pr in range(heads_per_group // 2):
            pair = g * (heads_per_group // 2) + pr
            for c, (_, _, _, _, s_ref, _) in zip(st, chains):
                c["xp"] = c["xall"][:, pair * LANES:(pair + 1) * LANES]
                c["sp"] = s_ref[pair]
                c["spb"] = c["sp"].astype(BF16)
                c["ys"], c["us"], c["decs"] = [], [], []
            for r in range(2):
                for c, (d, *_) in zip(st, chains):
                    ln = SSD_HEADS * d + 2 * pair + r
                    colb = c["cum_col"][:, ln:ln + 1]
                    rowb = c["cum_row"][ln:ln + 1, :]
                    dtr = c["dtt"][ln:ln + 1, :]
                    tot_h = c["tot"][ln:ln + 1, :]
                    lmat = jnp.exp(jnp.where(c["mask"], colb - rowb, -jnp.inf))
                    w = (c["cb"] * lmat * dtr).astype(BF16)
                    e = (c["cgf"] * jnp.exp(colb)).astype(BF16)
                    c["ys"].append(jnp.dot(w, c["xp"], preferred_element_type=F32)
                                   + jnp.dot(e, c["spb"], preferred_element_type=F32))
                    wrow = dtr * jnp.exp(tot_h - rowb)
                    c["us"].append(jnp.dot((c["bgt"] * wrow).astype(BF16), c["xp"],
                                           preferred_element_type=F32))
                    c["decs"].append(jnp.exp(tot_h))
            for c, (_, _, _, _, s_ref, y_ref) in zip(st, chains):
                y_ref[:, pair * LANES:(pair + 1) * LANES] = jnp.where(
                    low_half, c["ys"][0], c["ys"][1])
                dec = jnp.where(low_half, c["decs"][0], c["decs"][1])
                s_ref[pair] = c["sp"] * dec + jnp.where(low_half, c["us"][0], c["us"][1])


def _ssd_kernel(xf_ref, dtf_ref, dttf_ref, xb_ref, dtb_ref, dttb_ref, arow_ref, acol_ref,
                yf_ref, yb_ref, s_ref):
    @pl.when(pl.program_id(0) == 0)
    def _():
        s_ref[...] = jnp.zeros_like(s_ref)

    chains = []
    for b in range(xf_ref.shape[0]):
        chains.append((0, xf_ref.at[b], dtf_ref.at[b], dttf_ref.at[b], s_ref.at[b, 0],
                       yf_ref.at[b]))
        chains.append((1, xb_ref.at[b], dtb_ref.at[b], dttb_ref.at[b], s_ref.at[b, 1],
                       yb_ref.at[b]))
    _ssd_chains(chains, arow_ref, acol_ref)


def _ssd(xc, dt, dtt, arow, acol):
    bsz, t, ch = xc.shape
    nc = t // SSD_CHUNK
    nctx = CTX // SSD_CHUNK

    def fwd(c):
        return c

    def bwd(c):
        return jnp.where(c < nctx, nctx - 1 - c, nc - 1 + nctx - c)

    def specs(order):
        return [pl.BlockSpec((bsz, SSD_CHUNK, ch), lambda c: (0, order(c), 0)),
                pl.BlockSpec((bsz, SSD_CHUNK, LANES), lambda c: (0, order(c), 0)),
                pl.BlockSpec((bsz, 2 * SSD_HEADS, SSD_CHUNK), lambda c: (0, 0, order(c)))]

    y_spec = lambda order: pl.BlockSpec((bsz, SSD_CHUNK, SSD_WIDTH),
                                        lambda c: (0, order(c), 0))
    y_shape = jax.ShapeDtypeStruct((bsz, t, SSD_WIDTH), F32)
    return pl.pallas_call(
        _ssd_kernel,
        grid=(nc,),
        in_specs=specs(fwd) + specs(bwd) + [_const_spec((1, LANES)),
                                            _const_spec((2 * SSD_HEADS, 1))],
        out_specs=[y_spec(fwd), y_spec(bwd)],
        out_shape=[y_shape, y_shape],
        scratch_shapes=[pltpu.VMEM((bsz, 2, SSD_HEADS // 2, SSD_STATE, LANES), F32)],
        compiler_params=_cparams(1),
        name="ssd_scan",
    )(xc, dt, dtt, xc, dt, dtt, arow, acol)


def _masked_q(q):
    lane = lax.broadcasted_iota(jnp.int32, (1, LANES), 1)
    comp0 = lane < DA_HEAD_DIM
    zero = jnp.zeros_like(q)
    return jnp.where(comp0, q, zero), jnp.where(comp0, zero, q)


def _diff_out(lam_init, acc0, acc1, lv_ref, sg_ref):
    lv = lv_ref[...]
    lam = (jnp.exp(jnp.sum(lv[0:1] * lv[1:2], axis=1, keepdims=True))
           - jnp.exp(jnp.sum(lv[2:3] * lv[3:4], axis=1, keepdims=True)) + lam_init)
    o = acc0[:, :LANES] / acc0[:, LANES:] - lam * (acc1[:, :LANES] / acc1[:, LANES:])
    return (_rms(o, sg_ref[...]) * (1.0 - lam_init)).astype(BF16)


def _attn_ctx_kernel(lam_init, q_ref, k_ref, v_ref, lv_ref, sg_ref, o_ref):
    kc = k_ref[...]
    vext = jnp.concatenate([v_ref[...], jnp.ones((CTX, LANES), BF16)], axis=1)
    accs = []
    for qm in _masked_q(q_ref[...]):
        s = lax.dot_general(qm, kc, _NT, preferred_element_type=F32)
        p = jnp.exp2(s - jnp.max(s, axis=1, keepdims=True)).astype(BF16)
        accs.append(jnp.dot(p, vext, preferred_element_type=F32))
    o_ref[...] = _diff_out(lam_init, accs[0], accs[1], lv_ref, sg_ref)


def _attn_lat_kernel(lam_init, qa_ref, qb_ref, k_ref, v_ref, lv_ref, sg_ref, o_ref,
                     vext_ref, acc_ref, m_ref, qm_ref, p_ref, al_ref, *s_bufs):
    t = k_ref.shape[0]
    assert t % ATT_TK == 0 and ATT_TAIL % ATT_KT == 0 and len(s_bufs) == ATT_SLOTS

    @pl.when(pl.program_id(2) == 0)
    def _():
        vext_ref[:, :LANES] = v_ref[...]
        vext_ref[:, LANES:] = jnp.ones((t, LANES), BF16)

    half = ATT_TQ // 2
    for c, (qa, qb) in enumerate(zip(_masked_q(qa_ref[...]), _masked_q(qb_ref[...]))):
        qm_ref[c * ATT_TQ:c * ATT_TQ + half, :] = qa
        qm_ref[c * ATT_TQ + half:(c + 1) * ATT_TQ, :] = qb
    acc_ref[...] = jnp.zeros_like(acc_ref)
    m_ref[...] = jnp.full(m_ref.shape, NEG_BIG, F32)

    bounds = list(range(0, t, ATT_TK)) + [t - ATT_TAIL, t]
    n = len(bounds) - 1
    rows_all = 2 * ATT_TQ
    nseg = rows_all // ATT_PV_ROWS

    def key_tiles(j):
        return (bounds[j + 1] - bounds[j]) // ATT_KT if j < n else 0

    def scores_piece(j, kt0, kt1):
        lo, hi = kt0 * ATT_KT, kt1 * ATT_KT
        s_bufs[j % ATT_SLOTS][:, lo:hi] = lax.dot_general(
            qm_ref[...], k_ref[bounds[j] + lo:bounds[j] + hi, :], _NT,
            preferred_element_type=F32)

    def softmax_piece(j, r):
        rows = slice(r, r + ATT_RB)
        size = bounds[j + 1] - bounds[j]
        s = s_bufs[j % ATT_SLOTS][rows, :size]
        m_prev = m_ref[rows, :]
        m_next = jnp.maximum(m_prev, jnp.max(s, axis=1, keepdims=True))
        al_ref[rows, :] = jnp.exp2(m_prev - m_next)
        m_ref[rows, :] = m_next
        m_wide = jnp.concatenate([m_next] * (size // LANES), axis=1)
        p_ref[rows, :size] = jnp.exp2(s - m_wide).astype(BF16)

    def weighted_piece(j, seg):
        rows = slice(seg * ATT_PV_ROWS, (seg + 1) * ATT_PV_ROWS)
        size = bounds[j + 1] - bounds[j]
        pv = jnp.dot(p_ref[rows, :size], vext_ref[bounds[j]:bounds[j + 1], :],
                     preferred_element_type=F32)
        al = al_ref[rows, :]
        acc_ref[rows, :] = acc_ref[rows, :] * jnp.concatenate([al, al], axis=1) + pv

    scores_piece(0, 0, key_tiles(0))
    for j in range(n):
        tiles_next = key_tiles(j + 1)
        for seg in range(nseg):
            for r in range(seg * ATT_PV_ROWS, (seg + 1) * ATT_PV_ROWS, ATT_RB):
                softmax_piece(j, r)
            kt0, kt1 = seg * tiles_next // nseg, (seg + 1) * tiles_next // nseg
            if kt1 > kt0:
                scores_piece(j + 1, kt0, kt1)
            weighted_piece(j, seg)
    o_ref[...] = _diff_out(lam_init, acc_ref[:ATT_TQ, :], acc_ref[ATT_TQ:, :], lv_ref, sg_ref)


def _attention(q, k, v, lam_vec, sub_g, lam_init):
    bsz, t, _ = q.shape
    seq = t - CTX
    part = functools.partial
    consts = [_const_spec((4, DA_HEAD_DIM)), _const_spec((1, DA_V_DIM))]
    ctx_blk = pl.BlockSpec((None, CTX, LANES), lambda b, h: (b, 0, h))
    o_ctx = pl.pallas_call(
        part(_attn_ctx_kernel, lam_init),
        grid=(bsz, DA_HEADS),
        in_specs=[ctx_blk, ctx_blk, ctx_blk] + consts,
        out_specs=ctx_blk,
        out_shape=jax.ShapeDtypeStruct((bsz, CTX, DA_WIDTH), BF16),
        compiler_params=_cparams(2),
        name="diff_attn_ctx",
    )(q, k, v, lam_vec, sub_g)

    assert ATT_TQ == 2 * CTX
    kv = pl.BlockSpec((None, t, LANES), lambda b, h, i: (b, 0, h),
                      pipeline_mode=pl.Buffered(1))
    qa = pl.BlockSpec((None, CTX, LANES), lambda b, h, i: (b, 2 * i + 1, h))
    qb = pl.BlockSpec((None, CTX, LANES), lambda b, h, i: (b, 2 * i + 2, h))
    s_buf = pltpu.VMEM((2 * ATT_TQ, ATT_TK), F32)
    p_buf = pltpu.VMEM((2 * ATT_TQ, ATT_TK), BF16)
    al_buf = pltpu.VMEM((2 * ATT_TQ, LANES), F32)
    o_lat = pl.pallas_call(
        part(_attn_lat_kernel, lam_init),
        grid=(bsz, DA_HEADS, seq // ATT_TQ),
        in_specs=[qa, qb, kv, kv] + consts,
        out_specs=pl.BlockSpec((None, ATT_TQ, LANES), lambda b, h, i: (b, i, h)),
        out_shape=jax.ShapeDtypeStruct((bsz, seq, DA_WIDTH), BF16),
        scratch_shapes=[pltpu.VMEM((t, 2 * LANES), BF16),
                        pltpu.VMEM((2 * ATT_TQ, 2 * LANES), F32),
                        pltpu.VMEM((2 * ATT_TQ, LANES), F32),
                        pltpu.VMEM((2 * ATT_TQ, LANES), BF16), p_buf, al_buf]
                       + [s_buf] * ATT_SLOTS,
        compiler_params=_cparams(3),
        name="diff_attn",
    )(q, q, k, v, lam_vec, sub_g)
    return o_ctx, o_lat


def _mixout_kernel(x_ref, actx_ref, *refs):
    alat_refs = refs[:MIX_BLOCKS]
    (yf_ref, yb_ref, xc_ref, z_ref, dsk_ref, sg_ref, g_ref, gate_ref, wa_ref, ws_ref,
     o_ref) = refs[MIX_BLOCKS:]
    first_step = pl.program_id(1) == 0
    for k in range(MIX_BLOCKS):
        rows = slice(k * CTX, (k + 1) * CTX)
        attn = alat_refs[k][...]
        gate = gate_ref[0:1, :]
        if k == 0:
            attn = jnp.where(first_step, actx_ref[...], attn)
            gate = jnp.where(first_step, gate_ref[1:2, :], gate)
        y = yf_ref[rows, :] + yb_ref[rows, :] + dsk_ref[...] * xc_ref[rows, :].astype(F32)
        u = y * _silu(z_ref[rows, :].astype(F32))
        ssd = _rms(u, sg_ref[...]).astype(BF16)
        mix = (jnp.dot(attn, wa_ref[...], preferred_element_type=F32)
               + jnp.dot(ssd, ws_ref[...], preferred_element_type=F32))
        o_ref[rows, :] = x_ref[rows, :] + gate * _rms(mix, g_ref[...])


def _mixout(xx, attn_ctx, attn_lat, yf, yb, xc, z, dsk, ssd_g, g, gate, wa, ws):
    bsz, t, _ = xx.shape
    tile = MIX_BLOCKS * CTX
    assert t % tile == 0
    row = lambda w: pl.BlockSpec((None, tile, w), lambda b, i: (b, i, 0))
    mod = pl.BlockSpec((None, 2, D_MODEL), lambda b, i: (b, 0, 0))

    def lat_block(k):
        return pl.BlockSpec((None, CTX, DA_WIDTH),
                            lambda b, i: (b, jnp.maximum(MIX_BLOCKS * i + k - 1, 0), 0))

    return pl.pallas_call(
        _mixout_kernel,
        grid=(bsz, t // tile),
        in_specs=[row(D_MODEL), pl.BlockSpec((None, CTX, DA_WIDTH), lambda b, i: (b, 0, 0))]
                 + [lat_block(k) for k in range(MIX_BLOCKS)]
                 + [row(SSD_WIDTH), row(SSD_WIDTH),
                    row(SSD_WIDTH),
                    row(SSD_WIDTH), _const_spec((1, SSD_WIDTH)), _const_spec((1, SSD_WIDTH)),
                    _const_spec((1, D_MODEL)), mod, _const_spec(wa.shape),
                    _const_spec(ws.shape)],
        out_specs=row(D_MODEL),
        out_shape=jax.ShapeDtypeStruct(xx.shape, F32),
        compiler_params=_cparams(2),
        name="mix_out",
    )(xx, attn_ctx, *([attn_lat] * MIX_BLOCKS), yf, yb, xc, z, dsk, ssd_g, g, gate, wa, ws)


def _ffn_kernel(x_ref, gin_ref, sh_ref, sc_ref, gout_ref, gate_ref, wg_ref, wu_ref, wo_ref,
                o_ref):
    tile = x_ref.shape[0]
    is_ctx_all = _row_is_ctx(pl.program_id(1), tile)
    blocks = [slice(r0, r0 + FFN_ROWS) for r0 in range(0, tile, FFN_ROWS)]
    pick = lambda r, rows: jnp.where(is_ctx_all[rows, :], r[1:2, :], r[0:1, :])
    hs = [(_rms(x_ref[rows, :], gin_ref[...]) * (1.0 + pick(sc_ref, rows))
           + pick(sh_ref, rows)).astype(BF16) for rows in blocks]
    accs = [None] * len(blocks)
    for c in range(D_FF // FFN_CHUNK):
        sl = slice(c * FFN_CHUNK, (c + 1) * FFN_CHUNK)
        for b, h in enumerate(hs):
            gt = jnp.dot(h, wg_ref[:, sl], preferred_element_type=F32)
            up = jnp.dot(h, wu_ref[:, sl], preferred_element_type=F32)
            a = (_silu(gt) * up).astype(BF16)
            part = jnp.dot(a, wo_ref[sl, :], preferred_element_type=F32)
            accs[b] = part if accs[b] is None else accs[b] + part
    for rows, acc in zip(blocks, accs):
        o_ref[rows, :] = x_ref[rows, :] + pick(gate_ref, rows) * _rms(acc, gout_ref[...])


def _ffn(xx, gin, sh, sc, gout, gate, wg, wu, wo):
    bsz, t, _ = xx.shape
    nt = t // ROW_TILE
    row = pl.BlockSpec((None, ROW_TILE, D_MODEL), lambda b, i: (b, i, 0))
    mod = pl.BlockSpec((None, 2, D_MODEL), lambda b, i: (b, 0, 0))
    return pl.pallas_call(
        _ffn_kernel,
        grid=(bsz, nt),
        in_specs=[row, _const_spec((1, D_MODEL)), mod, mod, _const_spec((1, D_MODEL)), mod,
                  _const_spec(wg.shape), _const_spec(wu.shape), _const_spec(wo.shape)],
        out_specs=row,
        out_shape=jax.ShapeDtypeStruct(xx.shape, F32),
        compiler_params=_cparams(2),
        name="ffn",
    )(xx, gin, sh, sc, gout, gate, wg, wu, wo)


def _rope_tables(seq):
    rows = seq // GRID_W
    row = jnp.repeat(jnp.arange(rows, dtype=F32), GRID_W)
    col = jnp.tile(jnp.arange(GRID_W, dtype=F32), rows)
    nf = DA_HEAD_DIM // 4
    inv = ROPE_BASE ** (-jnp.arange(nf, dtype=F32) / nf)
    ang = jnp.concatenate([row[:, None] * inv, col[:, None] * inv], axis=-1)
    cos = jnp.concatenate([jnp.ones((CTX, DA_HEAD_DIM // 2), F32), jnp.cos(ang)], axis=0)
    sin = jnp.concatenate([jnp.zeros((CTX, DA_HEAD_DIM // 2), F32), jnp.sin(ang)], axis=0)
    return jnp.tile(cos, (1, 4)), jnp.tile(jnp.concatenate([-sin, sin], axis=1), (1, 2))


def kernel(x, c, ctx, c_ctx, w_ada, b_ada, norm_g, w_in, conv_w, conv_b, a_log, dt_bias,
           d_skip, ssd_norm_g, diff_lambda, subln_g, w_out, w_ffn_in, w_ffn_out):
    bsz, seq, _ = x.shape
    depth = w_ada.shape[0]
    t = CTX + seq
    assert ctx.shape[1] == CTX and t % ROW_TILE == 0 and seq % ATT_TQ == 0

    xx = jnp.concatenate([ctx, x], axis=1)
    cond = jnp.zeros((SUBLANES, D_MODEL), F32).at[:bsz].set(c).at[bsz].set(c_ctx)
    mod = _modulation(cond, w_ada, b_ada)
    cos_t, sin_t = _rope_tables(seq)

    q_end = DA_WIDTH
    k_end = 2 * DA_WIDTH
    v_end = 3 * DA_WIDTH
    z_end = v_end + SSD_WIDTH
    xbc_end = z_end + SSD_CONV_CH

    for i in range(depth):
        lam_init = 0.8 - 0.6 * math.exp(-0.3 * i)
        mods = [jnp.stack([mod[i, :bsz, j * D_MODEL:(j + 1) * D_MODEL],
                           jnp.broadcast_to(mod[i, bsz, j * D_MODEL:(j + 1) * D_MODEL],
                                            (bsz, D_MODEL))], axis=1) for j in range(N_MOD)]
        sh1, sc1, g1, sh2, sc2, g2 = mods
        wi = w_in[i].astype(BF16)
        wdt = jnp.zeros((D_MODEL, LANES), BF16).at[:, :2 * SSD_HEADS].set(wi[:, xbc_end:])
        lane16 = lambda a: jnp.zeros((1, LANES), F32).at[0, :2 * SSD_HEADS].set(a.reshape(-1))
        q, k, v, z, xbc, dt, dtt = _inproj(
            xx, norm_g[i, 0][None], sh1, sc1, cos_t, sin_t, lane16(dt_bias[i]),
            wi[:, :q_end], wi[:, q_end:k_end], wi[:, k_end:v_end], wi[:, v_end:z_end],
            wi[:, z_end:xbc_end], wdt)
        cw = jnp.zeros((SUBLANES, SSD_CONV_CH), F32).at[:SSD_CONV].set(conv_w[i])
        xc = _conv(xbc, cw, conv_b[i][None])
        yf, yb = _ssd(xc, dt, dtt, lane16(a_log[i]), a_log[i].reshape(2 * SSD_HEADS, 1))
        attn_ctx, attn_lat = _attention(q, k, v, diff_lambda[i], subln_g[i][None], lam_init)
        wo = w_out[i].astype(BF16)
        xx = _mixout(xx, attn_ctx, attn_lat, yf, yb, xc, z,
                     jnp.repeat(d_skip[i], SSD_HEAD_DIM)[None], ssd_norm_g[i][None],
                     norm_g[i, 1][None], g1, wo[:DA_WIDTH], wo[DA_WIDTH:])
        wf = w_ffn_in[i].astype(BF16)
        xx = _ffn(xx, norm_g[i, 2][None], sh2, sc2, norm_g[i, 3][None], g2,
                  wf[:, :D_FF], wf[:, D_FF:], w_ffn_out[i].astype(BF16))
    return xx[:, CTX:]
```

```python
import functools
import math

import jax
import jax.numpy as jnp
from jax import lax
from jax.experimental import pallas as pl
from jax.experimental.pallas import tpu as pltpu

F32 = jnp.float32
BF16 = jnp.bfloat16

D_MODEL = 1024
N_MOD = 6
CTX = 256
GRID_W = 64
DA_HEADS = 4
DA_HEAD_DIM = 64
DA_V_DIM = 2 * DA_HEAD_DIM
DA_WIDTH = DA_HEADS * DA_V_DIM
SSD_HEADS = 8
SSD_HEAD_DIM = 64
SSD_WIDTH = SSD_HEADS * SSD_HEAD_DIM
SSD_GROUPS = 2
SSD_STATE = 128
SSD_CONV = 5
SSD_CHUNK = 128
SSD_CONV_CH = SSD_WIDTH + 2 * SSD_GROUPS * SSD_STATE
D_FF = 2816
ROPE_BASE = 10000.0
EPS = 1e-6

LANES = 128
SUBLANES = 8
MOD_TILE = 1536
ROW_TILE = 640
CONV_TILE = 256
CONV_HALO = 16
ATT_TQ = 512
ATT_TK = 1280
ATT_SLOTS = 3
ATT_TAIL = 256
ATT_RB = 16
ATT_KT = 256
ATT_PV_ROWS = 512
FFN_BOUNDS = (0, 1536, D_FF)
FFN_ROWS = 320
INPROJ_ROWS = 384
MIX_BLOCKS = 5
VMEM_LIMIT = 56 * 1024 * 1024
LOG2E = 1.4426950408889634
NEG_BIG = -1e30


def _cparams(n_axes):
    return pltpu.CompilerParams(
        dimension_semantics=("arbitrary",) * n_axes, vmem_limit_bytes=VMEM_LIMIT)


def _rms(u, g):
    return u * lax.rsqrt(jnp.mean(u * u, axis=-1, keepdims=True) + EPS) * g


def _silu(u):
    return u * jax.nn.sigmoid(u)


def _row_is_ctx(step, tile):
    rows = step * tile + lax.broadcasted_iota(jnp.int32, (tile, 1), 0)
    return rows < CTX


_NT = (((1,), (1,)), ((), ()))


def _const_spec(shape):
    zeros = (0,) * len(shape)
    return pl.BlockSpec(shape, lambda *_: zeros, pipeline_mode=pl.Buffered(1))


def _mod_kernel(c_ref, w_ref, b_ref, o_ref):
    s = _silu(c_ref[...])
    o_ref[...] = jnp.dot(s, w_ref[...], precision=lax.Precision.HIGHEST,
                         preferred_element_type=F32) + b_ref[...]


def _modulation(cond, w_ada, b_ada):
    depth = w_ada.shape[0]
    tn = MOD_TILE
    return pl.pallas_call(
        _mod_kernel,
        grid=(depth, N_MOD * D_MODEL // tn),
        in_specs=[pl.BlockSpec((SUBLANES, D_MODEL), lambda i, j: (0, 0)),
                  pl.BlockSpec((None, D_MODEL, tn), lambda i, j: (i, 0, j)),
                  pl.BlockSpec((None, 1, tn), lambda i, j: (i, 0, j))],
        out_specs=pl.BlockSpec((None, SUBLANES, tn), lambda i, j: (i, 0, j)),
        out_shape=jax.ShapeDtypeStruct((depth, SUBLANES, N_MOD * D_MODEL), F32),
        compiler_params=_cparams(2),
        name="adaln_mod",
    )(cond, w_ada, b_ada.reshape(depth, 1, N_MOD * D_MODEL))


def _inproj_kernel(x_ref, g_ref, sh_ref, sc_ref, cos_ref, sin_ref, dtb_ref,
                   wq_ref, wk_ref, wv_ref, wz_ref, wx_ref, wdt_ref,
                   q_ref, k_ref, v_ref, z_ref, xbc_ref, dt_ref, dtt_ref):
    tile = x_ref.shape[0]
    is_ctx_all = _row_is_ctx(pl.program_id(1), tile)
    lane = lax.broadcasted_iota(jnp.int32, (1, LANES), 1)
    first_half = (lane % DA_HEAD_DIM) < (DA_HEAD_DIM // 2)
    q_scale = DA_HEAD_DIM ** -0.5 * LOG2E

    def rope(u, rows):
        partner = jnp.where(first_half,
                            pltpu.roll(u, LANES - DA_HEAD_DIM // 2, axis=1),
                            pltpu.roll(u, DA_HEAD_DIM // 2, axis=1))
        return u * cos_ref[rows, :] + partner * sin_ref[rows, :]

    blocks = [slice(r0, min(r0 + INPROJ_ROWS, tile)) for r0 in range(0, tile, INPROJ_ROWS)]
    hs = []
    for rows in blocks:
        is_ctx = is_ctx_all[rows, :]
        shift = jnp.where(is_ctx, sh_ref[1:2, :], sh_ref[0:1, :])
        scale = jnp.where(is_ctx, sc_ref[1:2, :], sc_ref[0:1, :])
        hs.append((_rms(x_ref[rows, :], g_ref[...]) * (1.0 + scale) + shift).astype(BF16))
    for rows, h in zip(blocks, hs):
        q = jnp.dot(h, wq_ref[...], preferred_element_type=F32)
        for hd in range(DA_HEADS):
            sl = slice(hd * LANES, (hd + 1) * LANES)
            q_ref[rows, sl] = (rope(q[:, sl], rows) * q_scale).astype(BF16)
    for rows, h in zip(blocks, hs):
        k = jnp.dot(h, wk_ref[...], preferred_element_type=F32)
        for hd in range(DA_HEADS):
            sl = slice(hd * LANES, (hd + 1) * LANES)
            k_ref[rows, sl] = rope(k[:, sl], rows).astype(BF16)
    for w_ref, out_ref in ((wv_ref, v_ref), (wz_ref, z_ref), (wx_ref, xbc_ref)):
        for rows, h in zip(blocks, hs):
            out_ref[rows, :] = jnp.dot(h, w_ref[...], preferred_element_type=F32).astype(BF16)
    for rows, h in zip(blocks, hs):
        dt_raw = jnp.dot(h, wdt_ref[...], preferred_element_type=F32) + dtb_ref[...]
        dt = jnp.maximum(dt_raw, 0.0) + jnp.log1p(jnp.exp(-jnp.abs(dt_raw)))
        dt_ref[rows, :] = dt
        dtt_ref[:, rows] = dt.T[:2 * SSD_HEADS, :]


def _inproj(xx, g, sh, sc, cos_t, sin_t, dtb, wq, wk, wv, wz, wx, wdt):
    bsz, t, _ = xx.shape
    nt = t // ROW_TILE
    row = lambda w: pl.BlockSpec((None, ROW_TILE, w), lambda b, i: (b, i, 0))
    mod = pl.BlockSpec((None, 2, D_MODEL), lambda b, i: (b, 0, 0))
    tab = pl.BlockSpec((ROW_TILE, LANES), lambda b, i: (i, 0))
    return pl.pallas_call(
        _inproj_kernel,
        grid=(bsz, nt),
        in_specs=[row(D_MODEL), _const_spec((1, D_MODEL)), mod, mod, tab, tab,
                  _const_spec((1, LANES)),
                  _const_spec(wq.shape), _const_spec(wk.shape), _const_spec(wv.shape),
                  _const_spec(wz.shape), _const_spec(wx.shape), _const_spec(wdt.shape)],
        out_specs=[row(DA_WIDTH), row(DA_WIDTH), row(DA_WIDTH), row(SSD_WIDTH),
                   row(SSD_CONV_CH), row(LANES),
                   pl.BlockSpec((None, 2 * SSD_HEADS, ROW_TILE), lambda b, i: (b, 0, i))],
        out_shape=[jax.ShapeDtypeStruct((bsz, t, DA_WIDTH), BF16),
                   jax.ShapeDtypeStruct((bsz, t, DA_WIDTH), BF16),
                   jax.ShapeDtypeStruct((bsz, t, DA_WIDTH), BF16),
                   jax.ShapeDtypeStruct((bsz, t, SSD_WIDTH), BF16),
                   jax.ShapeDtypeStruct((bsz, t, SSD_CONV_CH), BF16),
                   jax.ShapeDtypeStruct((bsz, t, LANES), F32),
                   jax.ShapeDtypeStruct((bsz, 2 * SSD_HEADS, t), F32)],
        compiler_params=_cparams(2),
        name="inproj",
    )(xx, g, sh, sc, cos_t, sin_t, dtb, wq, wk, wv, wz, wx, wdt)


def _conv_kernel(cur_ref, prev_ref, next_ref, w_ref, b_ref, o_ref, ext_ref):
    i = pl.program_id(1)
    n = pl.num_programs(1)
    half = SSD_CONV // 2
    pad = SUBLANES
    first_of_segment = jnp.logical_or(i == 0, i == CTX // CONV_TILE)
    last_of_segment = jnp.logical_or(i == CTX // CONV_TILE - 1, i == n - 1)
    prev = prev_ref[...].astype(F32)[CONV_HALO - pad:, :]
    nxt = next_ref[...].astype(F32)[:pad, :]
    ext_ref[0:pad, :] = jnp.where(first_of_segment, 0.0, prev)
    ext_ref[pad:pad + CONV_TILE, :] = cur_ref[...].astype(F32)
    ext_ref[pad + CONV_TILE:, :] = jnp.where(last_of_segment, 0.0, nxt)
    acc = b_ref[...] + w_ref[half:half + 1, :] * ext_ref[pad:pad + CONV_TILE, :]
    for s in range(-half, half + 1):
        if s != 0:
            acc = acc + w_ref[half + s:half + s + 1, :] * ext_ref[pl.ds(pad + s, CONV_TILE), :]
    o_ref[...] = _silu(acc).astype(BF16)


def _conv(xbc, w, b):
    bsz, t, ch = xbc.shape
    nt = t // CONV_TILE
    per = CONV_TILE // CONV_HALO
    last_halo = t // CONV_HALO - 1
    return pl.pallas_call(
        _conv_kernel,
        grid=(bsz, nt),
        in_specs=[pl.BlockSpec((None, CONV_TILE, ch), lambda b_, i: (b_, i, 0)),
                  pl.BlockSpec((None, CONV_HALO, ch),
                               lambda b_, i: (b_, jnp.maximum(i * per - 1, 0), 0)),
                  pl.BlockSpec((None, CONV_HALO, ch),
                               lambda b_, i: (b_, jnp.minimum((i + 1) * per, last_halo), 0)),
                  _const_spec((SUBLANES, ch)), _const_spec((1, ch))],
        out_specs=pl.BlockSpec((None, CONV_TILE, ch), lambda b_, i: (b_, i, 0)),
        out_shape=jax.ShapeDtypeStruct((bsz, t, ch), BF16),
        scratch_shapes=[pltpu.VMEM((CONV_TILE + 2 * SUBLANES, ch), F32)],
        compiler_params=_cparams(2),
        name="dwconv",
    )(xbc, xbc, xbc, w, b)


def _ssd_chains(chains, arow_ref, acol_ref):
    q = SSD_CHUNK
    hi = lax.Precision.HIGHEST
    ii = lax.broadcasted_iota(jnp.int32, (q, q), 0)
    jj = lax.broadcasted_iota(jnp.int32, (q, q), 1)
    lane = lax.broadcasted_iota(jnp.int32, (1, LANES), 1)
    low_half = lane < SSD_HEAD_DIM
    a_row = -jnp.exp(arow_ref[...])
    a_col = -jnp.exp(acol_ref[...])
    heads_per_group = SSD_HEADS // SSD_GROUPS
    st = []
    for d, xc_ref, dt_ref, dtt_ref, _, _ in chains:
        dtt = dtt_ref[...]
        dat = dtt * a_col
        mask = (jj <= ii) if d == 0 else (jj >= ii)
        mf = mask.astype(F32)
        st.append(dict(
            xall=xc_ref[...], dtt=dtt, mask=mask,
            cum_col=jnp.dot(mf, dt_ref[...] * a_row, precision=hi,
                            preferred_element_type=F32),
            cum_row=lax.dot_general(dat, mf, _NT, precision=hi,
                                    preferred_element_type=F32),
            tot=jnp.sum(dat, axis=1, keepdims=True)))
    for g in range(SSD_GROUPS):
        for c in st:
            bg = c["xall"][:, SSD_WIDTH + g * SSD_STATE:SSD_WIDTH + (g + 1) * SSD_STATE]
            c0 = SSD_WIDTH + SSD_GROUPS * SSD_STATE + g * SSD_STATE
            cg = c["xall"][:, c0:c0 + SSD_STATE]
            c["cb"] = lax.dot_general(cg, bg, _NT, preferred_element_type=F32)
            c["bgt"] = bg.astype(F32).T
            c["cgf"] = cg.astype(F32)
        for pr in range(heads_per_group // 2):
            pair = g * (heads_per_group // 2) + pr
            for c, (_, _, _, _, s_ref, _) in zip(st, chains):
                c["xp"] = c["xall"][:, pair * LANES:(pair + 1) * LANES]
                c["sp"] = s_ref[pair]
                c["spb"] = c["sp"].astype(BF16)
                c["ys"], c["us"], c["decs"] = [], [], []
            for r in range(2):
                for c, (d, *_) in zip(st, chains):
                    ln = SSD_HEADS * d + 2 * pair + r
                    colb = c["cum_col"][:, ln:ln + 1]
                    rowb = c["cum_row"][ln:ln + 1, :]
                    dtr = c["dtt"][ln:ln + 1, :]
                    tot_h = c["tot"][ln:ln + 1, :]
                    lmat = jnp.exp(jnp.where(c["mask"], colb - rowb, -jnp.inf))
                    w = (c["cb"] * lmat * dtr).astype(BF16)
                    e = (c["cgf"] * jnp.exp(colb)).astype(BF16)
                    c["ys"].append(jnp.dot(w, c["xp"], preferred_element_type=F32)
                                   + jnp.dot(e, c["spb"], preferred_element_type=F32))
                    wrow = dtr * jnp.exp(tot_h - rowb)
                    c["us"].append(jnp.dot((c["bgt"] * wrow).astype(BF16), c["xp"],
                                           preferred_element_type=F32))
                    c["decs"].append(jnp.exp(tot_h))
            for c, (_, _, _, _, s_ref, y_ref) in zip(st, chains):
                y_ref[:, pair * LANES:(pair + 1) * LANES] = jnp.where(
                    low_half, c["ys"][0], c["ys"][1])
                dec = jnp.where(low_half, c["decs"][0], c["decs"][1])
                s_ref[pair] = c["sp"] * dec + jnp.where(low_half, c["us"][0], c["us"][1])


def _ssd_kernel(xf_ref, dtf_ref, dttf_ref, xb_ref, dtb_ref, dttb_ref, arow_ref, acol_ref,
                yf_ref, yb_ref, s_ref):
    @pl.when(pl.program_id(0) == 0)
    def _():
        s_ref[...] = jnp.zeros_like(s_ref)

    chains = []
    for b in range(xf_ref.shape[0]):
        chains.append((0, xf_ref.at[b], dtf_ref.at[b], dttf_ref.at[b], s_ref.at[b, 0],
                       yf_ref.at[b]))
        chains.append((1, xb_ref.at[b], dtb_ref.at[b], dttb_ref.at[b], s_ref.at[b, 1],
                       yb_ref.at[b]))
    _ssd_chains(chains, arow_ref, acol_ref)


def _ssd(xc, dt, dtt, arow, acol):
    bsz, t, ch = xc.shape
    nc = t // SSD_CHUNK
    nctx = CTX // SSD_CHUNK

    def fwd(c):
        return c

    def bwd(c):
        return jnp.where(c < nctx, nctx - 1 - c, nc - 1 + nctx - c)

    def specs(order):
        return [pl.BlockSpec((bsz, SSD_CHUNK, ch), lambda c: (0, order(c), 0)),
                pl.BlockSpec((bsz, SSD_CHUNK, LANES), lambda c: (0, order(c), 0)),
                pl.BlockSpec((bsz, 2 * SSD_HEADS, SSD_CHUNK), lambda c: (0, 0, order(c)))]

    y_spec = lambda order: pl.BlockSpec((bsz, SSD_CHUNK, SSD_WIDTH),
                                        lambda c: (0, order(c), 0))
    y_shape = jax.ShapeDtypeStruct((bsz, t, SSD_WIDTH), F32)
    return pl.pallas_call(
        _ssd_kernel,
        grid=(nc,),
        in_specs=specs(fwd) + specs(bwd) + [_const_spec((1, LANES)),
                                            _const_spec((2 * SSD_HEADS, 1))],
        out_specs=[y_spec(fwd), y_spec(bwd)],
        out_shape=[y_shape, y_shape],
        scratch_shapes=[pltpu.VMEM((bsz, 2, SSD_HEADS // 2, SSD_STATE, LANES), F32)],
        compiler_params=_cparams(1),
        name="ssd_scan",
    )(xc, dt, dtt, xc, dt, dtt, arow, acol)


def _masked_q(q):
    lane = lax.broadcasted_iota(jnp.int32, (1, LANES), 1)
    comp0 = lane < DA_HEAD_DIM
    zero = jnp.zeros_like(q)
    return jnp.where(comp0, q, zero), jnp.where(comp0, zero, q)


def _diff_out(lam_init, acc0, acc1, lv_ref, sg_ref):
    lv = lv_ref[...]
    lam = (jnp.exp(jnp.sum(lv[0:1] * lv[1:2], axis=1, keepdims=True))
           - jnp.exp(jnp.sum(lv[2:3] * lv[3:4], axis=1, keepdims=True)) + lam_init)
    o = acc0[:, :LANES] / acc0[:, LANES:] - lam * (acc1[:, :LANES] / acc1[:, LANES:])
    return (_rms(o, sg_ref[...]) * (1.0 - lam_init)).astype(BF16)


def _attn_ctx_kernel(lam_init, q_ref, k_ref, v_ref, lv_ref, sg_ref, o_ref):
    kc = k_ref[...]
    vext = jnp.concatenate([v_ref[...], jnp.ones((CTX, LANES), BF16)], axis=1)
    accs = []
    for qm in _masked_q(q_ref[...]):
        s = lax.dot_general(qm, kc, _NT, preferred_element_type=F32)
        p = jnp.exp2(s - jnp.max(s, axis=1, keepdims=True)).astype(BF16)
        accs.append(jnp.dot(p, vext, preferred_element_type=F32))
    o_ref[...] = _diff_out(lam_init, accs[0], accs[1], lv_ref, sg_ref)


def _attn_lat_kernel(lam_init, qa_ref, qb_ref, k_ref, v_ref, lv_ref, sg_ref, o_ref,
                     vext_ref, acc_ref, m_ref, qm_ref, p_ref, al_ref, *s_bufs):
    t = k_ref.shape[0]
    assert t % ATT_TK == 0 and ATT_TAIL % ATT_KT == 0 and len(s_bufs) == ATT_SLOTS

    @pl.when(pl.program_id(2) == 0)
    def _():
        vext_ref[:, :LANES] = v_ref[...]
        vext_ref[:, LANES:] = jnp.ones((t, LANES), BF16)

    half = ATT_TQ // 2
    for c, (qa, qb) in enumerate(zip(_masked_q(qa_ref[...]), _masked_q(qb_ref[...]))):
        qm_ref[c * ATT_TQ:c * ATT_TQ + half, :] = qa
        qm_ref[c * ATT_TQ + half:(c + 1) * ATT_TQ, :] = qb
    acc_ref[...] = jnp.zeros_like(acc_ref)
    m_ref[...] = jnp.full(m_ref.shape, NEG_BIG, F32)

    bounds = list(range(0, t, ATT_TK)) + [t - ATT_TAIL, t]
    n = len(bounds) - 1
    rows_all = 2 * ATT_TQ
    nseg = rows_all // ATT_PV_ROWS

    def key_tiles(j):
        return (bounds[j + 1] - bounds[j]) // ATT_KT if j < n else 0

    def scores_piece(j, kt0, kt1):
        lo, hi = kt0 * ATT_KT, kt1 * ATT_KT
        s_bufs[j % ATT_SLOTS][:, lo:hi] = lax.dot_general(
            qm_ref[...], k_ref[bounds[j] + lo:bounds[j] + hi, :], _NT,
            preferred_element_type=F32)

    def softmax_piece(j, r):
        rows = slice(r, r + ATT_RB)
        size = bounds[j + 1] - bounds[j]
        s = s_bufs[j % ATT_SLOTS][rows, :size]
        m_prev = m_ref[rows, :]
        m_next = jnp.maximum(m_prev, jnp.max(s, axis=1, keepdims=True))
        al_ref[rows, :] = jnp.exp2(m_prev - m_next)
        m_ref[rows, :] = m_next
        m_wide = jnp.concatenate([m_next] * (size // LANES), axis=1)
        p_ref[rows, :size] = jnp.exp2(s - m_wide).astype(BF16)

    def weighted_piece(j, seg):
        rows = slice(seg * ATT_PV_ROWS, (seg + 1) * ATT_PV_ROWS)
        size = bounds[j + 1] - bounds[j]
        pv = jnp.dot(p_ref[rows, :size], vext_ref[bounds[j]:bounds[j + 1], :],
                     preferred_element_type=F32)
        al = al_ref[rows, :]
        acc_ref[rows, :] = acc_ref[rows, :] * jnp.concatenate([al, al], axis=1) + pv

    scores_piece(0, 0, key_tiles(0))
    for j in range(n):
        tiles_next = key_tiles(j + 1)
        for seg in range(nseg):
            for r in range(seg * ATT_PV_ROWS, (seg + 1) * ATT_PV_ROWS, ATT_RB):
                softmax_piece(j, r)
            kt0, kt1 = seg * tiles_next // nseg, (seg + 1) * tiles_next // nseg
            if kt1 > kt0:
                scores_piece(j + 1, kt0, kt1)
            weighted_piece(j, seg)
    o_ref[...] = _diff_out(lam_init, acc_ref[:ATT_TQ, :], acc_ref[ATT_TQ:, :], lv_ref, sg_ref)


def _attention(q, k, v, lam_vec, sub_g, lam_init):
    bsz, t, _ = q.shape
    seq = t - CTX
    part = functools.partial
    consts = [_const_spec((4, DA_HEAD_DIM)), _const_spec((1, DA_V_DIM))]
    ctx_blk = pl.BlockSpec((None, CTX, LANES), lambda b, h: (b, 0, h))
    o_ctx = pl.pallas_call(
        part(_attn_ctx_kernel, lam_init),
        grid=(bsz, DA_HEADS),
        in_specs=[ctx_blk, ctx_blk, ctx_blk] + consts,
        out_specs=ctx_blk,
        out_shape=jax.ShapeDtypeStruct((bsz, CTX, DA_WIDTH), BF16),
        compiler_params=_cparams(2),
        name="diff_attn_ctx",
    )(q, k, v, lam_vec, sub_g)

    assert ATT_TQ == 2 * CTX
    kv = pl.BlockSpec((None, t, LANES), lambda b, h, i: (b, 0, h),
                      pipeline_mode=pl.Buffered(1))
    qa = pl.BlockSpec((None, CTX, LANES), lambda b, h, i: (b, 2 * i + 1, h))
    qb = pl.BlockSpec((None, CTX, LANES), lambda b, h, i: (b, 2 * i + 2, h))
    s_buf = pltpu.VMEM((2 * ATT_TQ, ATT_TK), F32)
    p_buf = pltpu.VMEM((2 * ATT_TQ, ATT_TK), BF16)
    al_buf = pltpu.VMEM((2 * ATT_TQ, LANES), F32)
    o_lat = pl.pallas_call(
        part(_attn_lat_kernel, lam_init),
        grid=(bsz, DA_HEADS, seq // ATT_TQ),
        in_specs=[qa, qb, kv, kv] + consts,
        out_specs=pl.BlockSpec((None, ATT_TQ, LANES), lambda b, h, i: (b, i, h)),
        out_shape=jax.ShapeDtypeStruct((bsz, seq, DA_WIDTH), BF16),
        scratch_shapes=[pltpu.VMEM((t, 2 * LANES), BF16),
                        pltpu.VMEM((2 * ATT_TQ, 2 * LANES), F32),
                        pltpu.VMEM((2 * ATT_TQ, LANES), F32),
                        pltpu.VMEM((2 * ATT_TQ, LANES), BF16), p_buf, al_buf]
                       + [s_buf] * ATT_SLOTS,
        compiler_params=_cparams(3),
        name="diff_attn",
    )(q, q, k, v, lam_vec, sub_g)
    return o_ctx, o_lat


def _mixout_kernel(x_ref, actx_ref, *refs):
    alat_refs = refs[:MIX_BLOCKS]
    (yf_ref, yb_ref, xc_ref, z_ref, dsk_ref, sg_ref, g_ref, gate_ref, wa_ref, ws_ref,
     o_ref) = refs[MIX_BLOCKS:]
    first_step = pl.program_id(1) == 0
    for k in range(MIX_BLOCKS):
        rows = slice(k * CTX, (k + 1) * CTX)
        attn = alat_refs[k][...]
        gate = gate_ref[0:1, :]
        if k == 0:
            attn = jnp.where(first_step, actx_ref[...], attn)
            gate = jnp.where(first_step, gate_ref[1:2, :], gate)
        y = yf_ref[rows, :] + yb_ref[rows, :] + dsk_ref[...] * xc_ref[rows, :].astype(F32)
        u = y * _silu(z_ref[rows, :].astype(F32))
        ssd = _rms(u, sg_ref[...]).astype(BF16)
        mix = (jnp.dot(attn, wa_ref[...], preferred_element_type=F32)
               + jnp.dot(ssd, ws_ref[...], preferred_element_type=F32))
        o_ref[rows, :] = x_ref[rows, :] + gate * _rms(mix, g_ref[...])


def _mixout(xx, attn_ctx, attn_lat, yf, yb, xc, z, dsk, ssd_g, g, gate, wa, ws):
    bsz, t, _ = xx.shape
    tile = MIX_BLOCKS * CTX
    assert t % tile == 0
    row = lambda w: pl.BlockSpec((None, tile, w), lambda b, i: (b, i, 0))
    mod = pl.BlockSpec((None, 2, D_MODEL), lambda b, i: (b, 0, 0))

    def lat_block(k):
        return pl.BlockSpec((None, CTX, DA_WIDTH),
                            lambda b, i: (b, jnp.maximum(MIX_BLOCKS * i + k - 1, 0), 0))

    return pl.pallas_call(
        _mixout_kernel,
        grid=(bsz, t // tile),
        in_specs=[row(D_MODEL), pl.BlockSpec((None, CTX, DA_WIDTH), lambda b, i: (b, 0, 0))]
                 + [lat_block(k) for k in range(MIX_BLOCKS)]
                 + [row(SSD_WIDTH), row(SSD_WIDTH),
                    row(SSD_WIDTH),
                    row(SSD_WIDTH), _const_spec((1, SSD_WIDTH)), _const_spec((1, SSD_WIDTH)),
                    _const_spec((1, D_MODEL)), mod, _const_spec(wa.shape),
                    _const_spec(ws.shape)],
        out_specs=row(D_MODEL),
        out_shape=jax.ShapeDtypeStruct(xx.shape, F32),
        compiler_params=_cparams(2),
        name="mix_out",
    )(xx, attn_ctx, *([attn_lat] * MIX_BLOCKS), yf, yb, xc, z, dsk, ssd_g, g, gate, wa, ws)


def _ffn_kernel(x_ref, gin_ref, sh_ref, sc_ref, gout_ref, gate_ref, wg_ref, wu_ref, wo_ref,
                o_ref):
    tile = x_ref.shape[0]
    is_ctx_all = _row_is_ctx(pl.program_id(1), tile)
    blocks = [slice(r0, r0 + FFN_ROWS) for r0 in range(0, tile, FFN_ROWS)]
    pick = lambda r, rows: jnp.where(is_ctx_all[rows, :], r[1:2, :], r[0:1, :])
    hs = [(_rms(x_ref[rows, :], gin_ref[...]) * (1.0 + pick(sc_ref, rows))
           + pick(sh_ref, rows)).astype(BF16) for rows in blocks]
    accs = [None] * len(blocks)
    for c0, c1 in zip(FFN_BOUNDS[:-1], FFN_BOUNDS[1:]):
        sl = slice(c0, c1)
        for b, h in enumerate(hs):
            gt = jnp.dot(h, wg_ref[:, sl], preferred_element_type=F32)
            up = jnp.dot(h, wu_ref[:, sl], preferred_element_type=F32)
            a = (_silu(gt) * up).astype(BF16)
            part = jnp.dot(a, wo_ref[sl, :], preferred_element_type=F32)
            accs[b] = part if accs[b] is None else accs[b] + part
    for rows, acc in zip(blocks, accs):
        o_ref[rows, :] = x_ref[rows, :] + pick(gate_ref, rows) * _rms(acc, gout_ref[...])


def _ffn(xx, gin, sh, sc, gout, gate, wg, wu, wo):
    bsz, t, _ = xx.shape
    nt = t // ROW_TILE
    row = pl.BlockSpec((None, ROW_TILE, D_MODEL), lambda b, i: (b, i, 0))
    mod = pl.BlockSpec((None, 2, D_MODEL), lambda b, i: (b, 0, 0))
    return pl.pallas_call(
        _ffn_kernel,
        grid=(bsz, nt),
        in_specs=[row, _const_spec((1, D_MODEL)), mod, mod, _const_spec((1, D_MODEL)), mod,
                  _const_spec(wg.shape), _const_spec(wu.shape), _const_spec(wo.shape)],
        out_specs=row,
        out_shape=jax.ShapeDtypeStruct(xx.shape, F32),
        compiler_params=_cparams(2),
        name="ffn",
    )(xx, gin, sh, sc, gout, gate, wg, wu, wo)


def _rope_tables(seq):
    rows = seq // GRID_W
    row = jnp.repeat(jnp.arange(rows, dtype=F32), GRID_W)
    col = jnp.tile(jnp.arange(GRID_W, dtype=F32), rows)
    nf = DA_HEAD_DIM // 4
    inv = ROPE_BASE ** (-jnp.arange(nf, dtype=F32) / nf)
    ang = jnp.concatenate([row[:, None] * inv, col[:, None] * inv], axis=-1)
    cos = jnp.concatenate([jnp.ones((CTX, DA_HEAD_DIM // 2), F32), jnp.cos(ang)], axis=0)
    sin = jnp.concatenate([jnp.zeros((CTX, DA_HEAD_DIM // 2), F32), jnp.sin(ang)], axis=0)
    return jnp.tile(cos, (1, 4)), jnp.tile(jnp.concatenate([-sin, sin], axis=1), (1, 2))


def kernel(x, c, ctx, c_ctx, w_ada, b_ada, norm_g, w_in, conv_w, conv_b, a_log, dt_bias,
           d_skip, ssd_norm_g, diff_lambda, subln_g, w_out, w_ffn_in, w_ffn_out):
    bsz, seq, _ = x.shape
    depth = w_ada.shape[0]
    t = CTX + seq
    assert ctx.shape[1] == CTX and t % ROW_TILE == 0 and seq % ATT_TQ == 0

    xx = jnp.concatenate([ctx, x], axis=1)
    cond = jnp.zeros((SUBLANES, D_MODEL), F32).at[:bsz].set(c).at[bsz].set(c_ctx)
    mod = _modulation(cond, w_ada, b_ada)
    cos_t, sin_t = _rope_tables(seq)

    q_end = DA_WIDTH
    k_end = 2 * DA_WIDTH
    v_end = 3 * DA_WIDTH
    z_end = v_end + SSD_WIDTH
    xbc_end = z_end + SSD_CONV_CH

    for i in range(depth):
        lam_init = 0.8 - 0.6 * math.exp(-0.3 * i)
        mods = [jnp.stack([mod[i, :bsz, j * D_MODEL:(j + 1) * D_MODEL],
                           jnp.broadcast_to(mod[i, bsz, j * D_MODEL:(j + 1) * D_MODEL],
                                            (bsz, D_MODEL))], axis=1) for j in range(N_MOD)]
        sh1, sc1, g1, sh2, sc2, g2 = mods
        wi = w_in[i].astype(BF16)
        wdt = jnp.zeros((D_MODEL, LANES), BF16).at[:, :2 * SSD_HEADS].set(wi[:, xbc_end:])
        lane16 = lambda a: jnp.zeros((1, LANES), F32).at[0, :2 * SSD_HEADS].set(a.reshape(-1))
        q, k, v, z, xbc, dt, dtt = _inproj(
            xx, norm_g[i, 0][None], sh1, sc1, cos_t, sin_t, lane16(dt_bias[i]),
            wi[:, :q_end], wi[:, q_end:k_end], wi[:, k_end:v_end], wi[:, v_end:z_end],
            wi[:, z_end:xbc_end], wdt)
        cw = jnp.zeros((SUBLANES, SSD_CONV_CH), F32).at[:SSD_CONV].set(conv_w[i])
        xc = _conv(xbc, cw, conv_b[i][None])
        yf, yb = _ssd(xc, dt, dtt, lane16(a_log[i]), a_log[i].reshape(2 * SSD_HEADS, 1))
        attn_ctx, attn_lat = _attention(q, k, v, diff_lambda[i], subln_g[i][None], lam_init)
        wo = w_out[i].astype(BF16)
        xx = _mixout(xx, attn_ctx, attn_lat, yf, yb, xc, z,
                     jnp.repeat(d_skip[i], SSD_HEAD_DIM)[None], ssd_norm_g[i][None],
                     norm_g[i, 1][None], g1, wo[:DA_WIDTH], wo[DA_WIDTH:])
        wf = w_ffn_in[i].astype(BF16)
        xx = _ffn(xx, norm_g[i, 2][None], sh2, sc2, norm_g[i, 3][None], g2,
                  wf[:, :D_FF], wf[:, D_FF:], w_ffn_out[i].astype(BF16))
    return xx[:, CTX:]
```

```python
import functools
import math

import jax
import jax.numpy as jnp
from jax import lax
from jax.experimental import pallas as pl
from jax.experimental.pallas import tpu as pltpu

F32 = jnp.float32
BF16 = jnp.bfloat16

D_MODEL = 1024
N_MOD = 6
CTX = 256
GRID_W = 64
DA_HEADS = 4
DA_HEAD_DIM = 64
DA_V_DIM = 2 * DA_HEAD_DIM
DA_WIDTH = DA_HEADS * DA_V_DIM
SSD_HEADS = 8
SSD_HEAD_DIM = 64
SSD_WIDTH = SSD_HEADS * SSD_HEAD_DIM
SSD_GROUPS = 2
SSD_STATE = 128
SSD_CONV = 5
SSD_CHUNK = 128
SSD_CONV_CH = SSD_WIDTH + 2 * SSD_GROUPS * SSD_STATE
D_FF = 2816
ROPE_BASE = 10000.0
EPS = 1e-6

LANES = 128
SUBLANES = 8
MOD_TILE = 1536
ROW_TILE = 640
CONV_TILE = 256
CONV_HALO = 16
ATT_TQ = 512
ATT_TK = 1280
ATT_SLOTS = 3
ATT_TAIL = 256
ATT_RB = 16
ATT_KT = 256
ATT_PV_ROWS = 512
FFN_BOUNDS = (0, 1536, D_FF)
FFN_ROWS = 320
INPROJ_ROWS = 384
MIX_BLOCKS = 5
VMEM_LIMIT = 56 * 1024 * 1024
LOG2E = 1.4426950408889634
NEG_BIG = -1e30


def _cparams(n_axes):
    return pltpu.CompilerParams(
        dimension_semantics=("arbitrary",) * n_axes, vmem_limit_bytes=VMEM_LIMIT)


def _rms(u, g):
    return u * lax.rsqrt(jnp.mean(u * u, axis=-1, keepdims=True) + EPS) * g


def _silu(u):
    return u * jax.nn.sigmoid(u)


def _row_is_ctx(step, tile):
    rows = step * tile + lax.broadcasted_iota(jnp.int32, (tile, 1), 0)
    return rows < CTX


_NT = (((1,), (1,)), ((), ()))


def _const_spec(shape):
    zeros = (0,) * len(shape)
    return pl.BlockSpec(shape, lambda *_: zeros, pipeline_mode=pl.Buffered(1))


def _mod_kernel(c_ref, w_ref, b_ref, o_ref):
    s = _silu(c_ref[...])
    o_ref[...] = jnp.dot(s, w_ref[...], precision=lax.Precision.HIGHEST,
                         preferred_element_type=F32) + b_ref[...]


def _modulation(cond, w_ada, b_ada):
    depth = w_ada.shape[0]
    tn = MOD_TILE
    return pl.pallas_call(
        _mod_kernel,
        grid=(depth, N_MOD * D_MODEL // tn),
        in_specs=[pl.BlockSpec((SUBLANES, D_MODEL), lambda i, j: (0, 0)),
                  pl.BlockSpec((None, D_MODEL, tn), lambda i, j: (i, 0, j)),
                  pl.BlockSpec((None, 1, tn), lambda i, j: (i, 0, j))],
        out_specs=pl.BlockSpec((None, SUBLANES, tn), lambda i, j: (i, 0, j)),
        out_shape=jax.ShapeDtypeStruct((depth, SUBLANES, N_MOD * D_MODEL), F32),
        compiler_params=_cparams(2),
        name="adaln_mod",
    )(cond, w_ada, b_ada.reshape(depth, 1, N_MOD * D_MODEL))


def _inproj_kernel(x_ref, g_ref, sh_ref, sc_ref, cos_ref, sin_ref, dtb_ref,
                   wq_ref, wk_ref, wv_ref, wz_ref, wx_ref, wdt_ref,
                   q_ref, k_ref, v_ref, z_ref, xbc_ref, dt_ref, dtt_ref):
    tile = x_ref.shape[0]
    is_ctx_all = _row_is_ctx(pl.program_id(1), tile)
    lane = lax.broadcasted_iota(jnp.int32, (1, LANES), 1)
    first_half = (lane % DA_HEAD_DIM) < (DA_HEAD_DIM // 2)
    q_scale = DA_HEAD_DIM ** -0.5 * LOG2E

    def rope(u, rows):
        partner = jnp.where(first_half,
                            pltpu.roll(u, LANES - DA_HEAD_DIM // 2, axis=1),
                            pltpu.roll(u, DA_HEAD_DIM // 2, axis=1))
        return u * cos_ref[rows, :] + partner * sin_ref[rows, :]

    blocks = [slice(r0, min(r0 + INPROJ_ROWS, tile)) for r0 in range(0, tile, INPROJ_ROWS)]
    hs = []
    for rows in blocks:
        is_ctx = is_ctx_all[rows, :]
        shift = jnp.where(is_ctx, sh_ref[1:2, :], sh_ref[0:1, :])
        scale = jnp.where(is_ctx, sc_ref[1:2, :], sc_ref[0:1, :])
        hs.append((_rms(x_ref[rows, :], g_ref[...]) * (1.0 + scale) + shift).astype(BF16))
    for rows, h in zip(blocks, hs):
        q = jnp.dot(h, wq_ref[...], preferred_element_type=F32)
        for hd in range(DA_HEADS):
            sl = slice(hd * LANES, (hd + 1) * LANES)
            q_ref[rows, sl] = (rope(q[:, sl], rows) * q_scale).astype(BF16)
    for rows, h in zip(blocks, hs):
        k = jnp.dot(h, wk_ref[...], preferred_element_type=F32)
        for hd in range(DA_HEADS):
            sl = slice(hd * LANES, (hd + 1) * LANES)
            k_ref[rows, sl] = rope(k[:, sl], rows).astype(BF16)
    for w_ref, out_ref in ((wv_ref, v_ref), (wz_ref, z_ref), (wx_ref, xbc_ref)):
        for rows, h in zip(blocks, hs):
            out_ref[rows, :] = jnp.dot(h, w_ref[...], preferred_element_type=F32).astype(BF16)
    for rows, h in zip(blocks, hs):
        dt_raw = jnp.dot(h, wdt_ref[...], preferred_element_type=F32) + dtb_ref[...]
        dt = jnp.maximum(dt_raw, 0.0) + jnp.log1p(jnp.exp(-jnp.abs(dt_raw)))
        dt_ref[rows, :] = dt
        dtt_ref[:, rows] = dt.T[:2 * SSD_HEADS, :]


def _inproj(xx, g, sh, sc, cos_t, sin_t, dtb, wq, wk, wv, wz, wx, wdt):
    bsz, t, _ = xx.shape
    nt = t // ROW_TILE
    row = lambda w: pl.BlockSpec((None, ROW_TILE, w), lambda b, i: (b, i, 0))
    mod = pl.BlockSpec((None, 2, D_MODEL), lambda b, i: (b, 0, 0))
    tab = pl.BlockSpec((ROW_TILE, LANES), lambda b, i: (i, 0))
    return pl.pallas_call(
        _inproj_kernel,
        grid=(bsz, nt),
        in_specs=[row(D_MODEL), _const_spec((1, D_MODEL)), mod, mod, tab, tab,
                  _const_spec((1, LANES)),
                  _const_spec(wq.shape), _const_spec(wk.shape), _const_spec(wv.shape),
                  _const_spec(wz.shape), _const_spec(wx.shape), _const_spec(wdt.shape)],
        out_specs=[row(DA_WIDTH), row(DA_WIDTH), row(DA_WIDTH), row(SSD_WIDTH),
                   row(SSD_CONV_CH), row(LANES),
                   pl.BlockSpec((None, 2 * SSD_HEADS, ROW_TILE), lambda b, i: (b, 0, i))],
        out_shape=[jax.ShapeDtypeStruct((bsz, t, DA_WIDTH), BF16),
                   jax.ShapeDtypeStruct((bsz, t, DA_WIDTH), BF16),
                   jax.ShapeDtypeStruct((bsz, t, DA_WIDTH), BF16),
                   jax.ShapeDtypeStruct((bsz, t, SSD_WIDTH), BF16),
                   jax.ShapeDtypeStruct((bsz, t, SSD_CONV_CH), BF16),
                   jax.ShapeDtypeStruct((bsz, t, LANES), F32),
                   jax.ShapeDtypeStruct((bsz, 2 * SSD_HEADS, t), F32)],
        compiler_params=_cparams(2),
        name="inproj",
    )(xx, g, sh, sc, cos_t, sin_t, dtb, wq, wk, wv, wz, wx, wdt)


def _conv_kernel(cur_ref, prev_ref, next_ref, w_ref, b_ref, o_ref, ext_ref):
    i = pl.program_id(1)
    n = pl.num_programs(1)
    half = SSD_CONV // 2
    pad = SUBLANES
    first_of_segment = jnp.logical_or(i == 0, i == CTX // CONV_TILE)
    last_of_segment = jnp.logical_or(i == CTX // CONV_TILE - 1, i == n - 1)
    prev = prev_ref[...].astype(F32)[CONV_HALO - pad:, :]
    nxt = next_ref[...].astype(F32)[:pad, :]
    ext_ref[0:pad, :] = jnp.where(first_of_segment, 0.0, prev)
    ext_ref[pad:pad + CONV_TILE, :] = cur_ref[...].astype(F32)
    ext_ref[pad + CONV_TILE:, :] = jnp.where(last_of_segment, 0.0, nxt)
    acc = b_ref[...] + w_ref[half:half + 1, :] * ext_ref[pad:pad + CONV_TILE, :]
    for s in range(-half, half + 1):
        if s != 0:
            acc = acc + w_ref[half + s:half + s + 1, :] * ext_ref[pl.ds(pad + s, CONV_TILE), :]
    o_ref[...] = _silu(acc).astype(BF16)


def _conv(xbc, w, b):
    bsz, t, ch = xbc.shape
    nt = t // CONV_TILE
    per = CONV_TILE // CONV_HALO
    last_halo = t // CONV_HALO - 1
    return pl.pallas_call(
        _conv_kernel,
        grid=(bsz, nt),
        in_specs=[pl.BlockSpec((None, CONV_TILE, ch), lambda b_, i: (b_, i, 0)),
                  pl.BlockSpec((None, CONV_HALO, ch),
                               lambda b_, i: (b_, jnp.maximum(i * per - 1, 0), 0)),
                  pl.BlockSpec((None, CONV_HALO, ch),
                               lambda b_, i: (b_, jnp.minimum((i + 1) * per, last_halo), 0)),
                  _const_spec((SUBLANES, ch)), _const_spec((1, ch))],
        out_specs=pl.BlockSpec((None, CONV_TILE, ch), lambda b_, i: (b_, i, 0)),
        out_shape=jax.ShapeDtypeStruct((bsz, t, ch), BF16),
        scratch_shapes=[pltpu.VMEM((CONV_TILE + 2 * SUBLANES, ch), F32)],
        compiler_params=_cparams(2),
        name="dwconv",
    )(xbc, xbc, xbc, w, b)


def _ssd_chains(chains, arow_ref, acol_ref):
    q = SSD_CHUNK
    hi = lax.Precision.HIGHEST
    ii = lax.broadcasted_iota(jnp.int32, (q, q), 0)
    jj = lax.broadcasted_iota(jnp.int32, (q, q), 1)
    lane = lax.broadcasted_iota(jnp.int32, (1, LANES), 1)
    low_half = lane < SSD_HEAD_DIM
    a_row = -jnp.exp(arow_ref[...])
    a_col = -jnp.exp(acol_ref[...])
    heads_per_group = SSD_HEADS // SSD_GROUPS
    st = []
    for d, xc_ref, dt_ref, dtt_ref, _, _ in chains:
        dtt = dtt_ref[...]
        dat = dtt * a_col
        mask = (jj <= ii) if d == 0 else (jj >= ii)
        mf = mask.astype(F32)
        st.append(dict(
            xall=xc_ref[...], dtt=dtt, mask=mask,
            cum_col=jnp.dot(mf, dt_ref[...] * a_row, precision=hi,
                            preferred_element_type=F32),
            cum_row=lax.dot_general(dat, mf, _NT, precision=hi,
                                    preferred_element_type=F32),
            tot=jnp.sum(dat, axis=1, keepdims=True)))
    for g in range(SSD_GROUPS):
        for c in st:
            bg = c["xall"][:, SSD_WIDTH + g * SSD_STATE:SSD_WIDTH + (g + 1) * SSD_STATE]
            c0 = SSD_WIDTH + SSD_GROUPS * SSD_STATE + g * SSD_STATE
            cg = c["xall"][:, c0:c0 + SSD_STATE]
            c["cb"] = lax.dot_general(cg, bg, _NT, preferred_element_type=F32)
            c["bgt"] = bg.astype(F32).T
            c["cgf"] = cg.astype(F32)
        for pr in range(heads_per_group // 2):
            pair = g * (heads_per_group // 2) + pr
            for c, (_, _, _, _, s_ref, _) in zip(st, chains):
                c["xp"] = c["xall"][:, pair * LANES:(pair + 1) * LANES]
                c["sp"] = s_ref[pair]
                c["spb"] = c["sp"].astype(BF16)
                c["ys"], c["us"], c["decs"] = [], [], []
            for r in range(2):
                for c, (d, *_) in zip(st, chains):
                    ln = SSD_HEADS * d + 2 * pair + r
                    colb = c["cum_col"][:, ln:ln + 1]
                    rowb = c["cum_row"][ln:ln + 1, :]
                    dtr = c["dtt"][ln:ln + 1, :]
                    tot_h = c["tot"][ln:ln + 1, :]
                    lmat = jnp.exp(jnp.where(c["mask"], colb - rowb, -jnp.inf))
                    w = (c["cb"] * lmat * dtr).astype(BF16)
                    e = (c["cgf"] * jnp.exp(colb)).astype(BF16)
                    c["ys"].append(jnp.dot(w, c["xp"], preferred_element_type=F32)
                                   + jnp.dot(e, c["spb"], preferred_element_type=F32))
                    wrow = dtr * jnp.exp(tot_h - rowb)
                    c["us"].append(jnp.dot((c["bgt"] * wrow).astype(BF16), c["xp"],
                                           preferred_element_type=F32))
                    c["decs"].append(jnp.exp(tot_h))
            for c, (_, _, _, _, s_ref, y_ref) in zip(st, chains):
                y_ref[:, pair * LANES:(pair + 1) * LANES] = jnp.where(
                    low_half, c["ys"][0], c["ys"][1]).astype(BF16)
                dec = jnp.where(low_half, c["decs"][0], c["decs"][1])
                s_ref[pair] = c["sp"] * dec + jnp.where(low_half, c["us"][0], c["us"][1])


def _ssd_kernel(xf_ref, dtf_ref, dttf_ref, xb_ref, dtb_ref, dttb_ref, arow_ref, acol_ref,
                yf_ref, yb_ref, s_ref):
    @pl.when(pl.program_id(0) == 0)
    def _():
        s_ref[...] = jnp.zeros_like(s_ref)

    chains = []
    for b in range(xf_ref.shape[0]):
        chains.append((0, xf_ref.at[b], dtf_ref.at[b], dttf_ref.at[b], s_ref.at[b, 0],
                       yf_ref.at[b]))
        chains.append((1, xb_ref.at[b], dtb_ref.at[b], dttb_ref.at[b], s_ref.at[b, 1],
                       yb_ref.at[b]))
    _ssd_chains(chains, arow_ref, acol_ref)


def _ssd(xc, dt, dtt, arow, acol):
    bsz, t, ch = xc.shape
    nc = t // SSD_CHUNK
    nctx = CTX // SSD_CHUNK

    def fwd(c):
        return c

    def bwd(c):
        return jnp.where(c < nctx, nctx - 1 - c, nc - 1 + nctx - c)

    def specs(order):
        return [pl.BlockSpec((bsz, SSD_CHUNK, ch), lambda c: (0, order(c), 0)),
                pl.BlockSpec((bsz, SSD_CHUNK, LANES), lambda c: (0, order(c), 0)),
                pl.BlockSpec((bsz, 2 * SSD_HEADS, SSD_CHUNK), lambda c: (0, 0, order(c)))]

    y_spec = lambda order: pl.BlockSpec((bsz, SSD_CHUNK, SSD_WIDTH),
                                        lambda c: (0, order(c), 0))
    y_shape = jax.ShapeDtypeStruct((bsz, t, SSD_WIDTH), BF16)
    return pl.pallas_call(
        _ssd_kernel,
        grid=(nc,),
        in_specs=specs(fwd) + specs(bwd) + [_const_spec((1, LANES)),
                                            _const_spec((2 * SSD_HEADS, 1))],
        out_specs=[y_spec(fwd), y_spec(bwd)],
        out_shape=[y_shape, y_shape],
        scratch_shapes=[pltpu.VMEM((bsz, 2, SSD_HEADS // 2, SSD_STATE, LANES), F32)],
        compiler_params=_cparams(1),
        name="ssd_scan",
    )(xc, dt, dtt, xc, dt, dtt, arow, acol)


def _masked_q(q):
    lane = lax.broadcasted_iota(jnp.int32, (1, LANES), 1)
    comp0 = lane < DA_HEAD_DIM
    zero = jnp.zeros_like(q)
    return jnp.where(comp0, q, zero), jnp.where(comp0, zero, q)


def _diff_out(lam_init, acc0, acc1, lv_ref, sg_ref):
    lv = lv_ref[...]
    lam = (jnp.exp(jnp.sum(lv[0:1] * lv[1:2], axis=1, keepdims=True))
           - jnp.exp(jnp.sum(lv[2:3] * lv[3:4], axis=1, keepdims=True)) + lam_init)
    o = acc0[:, :LANES] / acc0[:, LANES:] - lam * (acc1[:, :LANES] / acc1[:, LANES:])
    return (_rms(o, sg_ref[...]) * (1.0 - lam_init)).astype(BF16)


def _attn_ctx_kernel(lam_init, q_ref, k_ref, v_ref, lv_ref, sg_ref, o_ref):
    kc = k_ref[...]
    vext = jnp.concatenate([v_ref[...], jnp.ones((CTX, LANES), BF16)], axis=1)
    accs = []
    for qm in _masked_q(q_ref[...]):
        s = lax.dot_general(qm, kc, _NT, preferred_element_type=F32)
        p = jnp.exp2(s - jnp.max(s, axis=1, keepdims=True)).astype(BF16)
        accs.append(jnp.dot(p, vext, preferred_element_type=F32))
    o_ref[...] = _diff_out(lam_init, accs[0], accs[1], lv_ref, sg_ref)


def _attn_lat_kernel(lam_init, qa_ref, qb_ref, k_ref, v_ref, lv_ref, sg_ref, o_ref,
                     vext_ref, acc_ref, m_ref, qm_ref, p_ref, al_ref, *s_bufs):
    t = k_ref.shape[0]
    assert t % ATT_TK == 0 and ATT_TAIL % ATT_KT == 0 and len(s_bufs) == ATT_SLOTS

    @pl.when(pl.program_id(2) == 0)
    def _():
        vext_ref[:, :LANES] = v_ref[...]
        vext_ref[:, LANES:] = jnp.ones((t, LANES), BF16)

    half = ATT_TQ // 2
    for c, (qa, qb) in enumerate(zip(_masked_q(qa_ref[...]), _masked_q(qb_ref[...]))):
        qm_ref[c * ATT_TQ:c * ATT_TQ + half, :] = qa
        qm_ref[c * ATT_TQ + half:(c + 1) * ATT_TQ, :] = qb
    acc_ref[...] = jnp.zeros_like(acc_ref)
    m_ref[...] = jnp.full(m_ref.shape, NEG_BIG, F32)

    bounds = list(range(0, t, ATT_TK)) + [t - ATT_TAIL, t]
    n = len(bounds) - 1
    rows_all = 2 * ATT_TQ
    nseg = rows_all // ATT_PV_ROWS

    def key_tiles(j):
        return (bounds[j + 1] - bounds[j]) // ATT_KT if j < n else 0

    def scores_piece(j, kt0, kt1):
        lo, hi = kt0 * ATT_KT, kt1 * ATT_KT
        s_bufs[j % ATT_SLOTS][:, lo:hi] = lax.dot_general(
            qm_ref[...], k_ref[bounds[j] + lo:bounds[j] + hi, :], _NT,
            preferred_element_type=F32)

    def softmax_piece(j, r):
        rows = slice(r, r + ATT_RB)
        size = bounds[j + 1] - bounds[j]
        s = s_bufs[j % ATT_SLOTS][rows, :size]
        m_prev = m_ref[rows, :]
        m_next = jnp.maximum(m_prev, jnp.max(s, axis=1, keepdims=True))
        al_ref[rows, :] = jnp.exp2(m_prev - m_next)
        m_ref[rows, :] = m_next
        m_wide = jnp.concatenate([m_next] * (size // LANES), axis=1)
        p_ref[rows, :size] = jnp.exp2(s - m_wide).astype(BF16)

    def weighted_piece(j, seg):
        rows = slice(seg * ATT_PV_ROWS, (seg + 1) * ATT_PV_ROWS)
        size = bounds[j + 1] - bounds[j]
        pv = jnp.dot(p_ref[rows, :size], vext_ref[bounds[j]:bounds[j + 1], :],
                     preferred_element_type=F32)
        al = al_ref[rows, :]
        acc_ref[rows, :] = acc_ref[rows, :] * jnp.concatenate([al, al], axis=1) + pv

    scores_piece(0, 0, key_tiles(0))
    for j in range(n):
        tiles_next = key_tiles(j + 1)
        for seg in range(nseg):
            for r in range(seg * ATT_PV_ROWS, (seg + 1) * ATT_PV_ROWS, ATT_RB):
                softmax_piece(j, r)
            kt0, kt1 = seg * tiles_next // nseg, (seg + 1) * tiles_next // nseg
            if kt1 > kt0:
                scores_piece(j + 1, kt0, kt1)
            weighted_piece(j, seg)
    o_ref[...] = _diff_out(lam_init, acc_ref[:ATT_TQ, :], acc_ref[ATT_TQ:, :], lv_ref, sg_ref)


def _attention(q, k, v, lam_vec, sub_g, lam_init):
    bsz, t, _ = q.shape
    seq = t - CTX
    part = functools.partial
    consts = [_const_spec((4, DA_HEAD_DIM)), _const_spec((1, DA_V_DIM))]
    ctx_blk = pl.BlockSpec((None, CTX, LANES), lambda b, h: (b, 0, h))
    o_ctx = pl.pallas_call(
        part(_attn_ctx_kernel, lam_init),
        grid=(bsz, DA_HEADS),
        in_specs=[ctx_blk, ctx_blk, ctx_blk] + consts,
        out_specs=ctx_blk,
        out_shape=jax.ShapeDtypeStruct((bsz, CTX, DA_WIDTH), BF16),
        compiler_params=_cparams(2),
        name="diff_attn_ctx",
    )(q, k, v, lam_vec, sub_g)

    assert ATT_TQ == 2 * CTX
    kv = pl.BlockSpec((None, t, LANES), lambda b, h, i: (b, 0, h),
                      pipeline_mode=pl.Buffered(1))
    qa = pl.BlockSpec((None, CTX, LANES), lambda b, h, i: (b, 2 * i + 1, h))
    qb = pl.BlockSpec((None, CTX, LANES), lambda b, h, i: (b, 2 * i + 2, h))
    s_buf = pltpu.VMEM((2 * ATT_TQ, ATT_TK), F32)
    p_buf = pltpu.VMEM((2 * ATT_TQ, ATT_TK), BF16)
    al_buf = pltpu.VMEM((2 * ATT_TQ, LANES), F32)
    o_lat = pl.pallas_call(
        part(_attn_lat_kernel, lam_init),
        grid=(bsz, DA_HEADS, seq // ATT_TQ),
        in_specs=[qa, qb, kv, kv] + consts,
        out_specs=pl.BlockSpec((None, ATT_TQ, LANES), lambda b, h, i: (b, i, h)),
        out_shape=jax.ShapeDtypeStruct((bsz, seq, DA_WIDTH), BF16),
        scratch_shapes=[pltpu.VMEM((t, 2 * LANES), BF16),
                        pltpu.VMEM((2 * ATT_TQ, 2 * LANES), F32),
                        pltpu.VMEM((2 * ATT_TQ, LANES), F32),
                        pltpu.VMEM((2 * ATT_TQ, LANES), BF16), p_buf, al_buf]
                       + [s_buf] * ATT_SLOTS,
        compiler_params=_cparams(3),
        name="diff_attn",
    )(q, q, k, v, lam_vec, sub_g)
    return o_ctx, o_lat


def _mixout_kernel(x_ref, actx_ref, *refs):
    alat_refs = refs[:MIX_BLOCKS]
    (yf_ref, yb_ref, xc_ref, z_ref, dsk_ref, sg_ref, g_ref, gate_ref, wa_ref, ws_ref,
     o_ref) = refs[MIX_BLOCKS:]
    first_step = pl.program_id(1) == 0
    for k in range(MIX_BLOCKS):
        rows = slice(k * CTX, (k + 1) * CTX)
        attn = alat_refs[k][...]
        gate = gate_ref[0:1, :]
        if k == 0:
            attn = jnp.where(first_step, actx_ref[...], attn)
            gate = jnp.where(first_step, gate_ref[1:2, :], gate)
        y = (yf_ref[rows, :].astype(F32) + yb_ref[rows, :].astype(F32)
             + dsk_ref[...] * xc_ref[rows, :].astype(F32))
        u = y * _silu(z_ref[rows, :].astype(F32))
        ssd = _rms(u, sg_ref[...]).astype(BF16)
        mix = (jnp.dot(attn, wa_ref[...], preferred_element_type=F32)
               + jnp.dot(ssd, ws_ref[...], preferred_element_type=F32))
        o_ref[rows, :] = x_ref[rows, :] + gate * _rms(mix, g_ref[...])


def _mixout(xx, attn_ctx, attn_lat, yf, yb, xc, z, dsk, ssd_g, g, gate, wa, ws):
    bsz, t, _ = xx.shape
    tile = MIX_BLOCKS * CTX
    assert t % tile == 0
    row = lambda w: pl.BlockSpec((None, tile, w), lambda b, i: (b, i, 0))
    mod = pl.BlockSpec((None, 2, D_MODEL), lambda b, i: (b, 0, 0))

    def lat_block(k):
        return pl.BlockSpec((None, CTX, DA_WIDTH),
                            lambda b, i: (b, jnp.maximum(MIX_BLOCKS * i + k - 1, 0), 0))

    return pl.pallas_call(
        _mixout_kernel,
        grid=(bsz, t // tile),
        in_specs=[row(D_MODEL), pl.BlockSpec((None, CTX, DA_WIDTH), lambda b, i: (b, 0, 0))]
                 + [lat_block(k) for k in range(MIX_BLOCKS)]
                 + [row(SSD_WIDTH), row(SSD_WIDTH),
                    row(SSD_WIDTH),
                    row(SSD_WIDTH), _const_spec((1, SSD_WIDTH)), _const_spec((1, SSD_WIDTH)),
                    _const_spec((1, D_MODEL)), mod, _const_spec(wa.shape),
                    _const_spec(ws.shape)],
        out_specs=row(D_MODEL),
        out_shape=jax.ShapeDtypeStruct(xx.shape, F32),
        compiler_params=_cparams(2),
        name="mix_out",
    )(xx, attn_ctx, *([attn_lat] * MIX_BLOCKS), yf, yb, xc, z, dsk, ssd_g, g, gate, wa, ws)


def _ffn_kernel(x_ref, gin_ref, sh_ref, sc_ref, gout_ref, gate_ref, wg_ref, wu_ref, wo_ref,
                o_ref):
    tile = x_ref.shape[0]
    is_ctx_all = _row_is_ctx(pl.program_id(1), tile)
    blocks = [slice(r0, r0 + FFN_ROWS) for r0 in range(0, tile, FFN_ROWS)]
    pick = lambda r, rows: jnp.where(is_ctx_all[rows, :], r[1:2, :], r[0:1, :])
    hs = [(_rms(x_ref[rows, :], gin_ref[...]) * (1.0 + pick(sc_ref, rows))
           + pick(sh_ref, rows)).astype(BF16) for rows in blocks]
    accs = [None] * len(blocks)
    for c0, c1 in zip(FFN_BOUNDS[:-1], FFN_BOUNDS[1:]):
        sl = slice(c0, c1)
        for b, h in enumerate(hs):
            gt = jnp.dot(h, wg_ref[:, sl], preferred_element_type=F32)
            up = jnp.dot(h, wu_ref[:, sl], preferred_element_type=F32)
            a = (_silu(gt) * up).astype(BF16)
            part = jnp.dot(a, wo_ref[sl, :], preferred_element_type=F32)
            accs[b] = part if accs[b] is None else accs[b] + part
    for rows, acc in zip(blocks, accs):
        o_ref[rows, :] = x_ref[rows, :] + pick(gate_ref, rows) * _rms(acc, gout_ref[...])


def _ffn(xx, gin, sh, sc, gout, gate, wg, wu, wo):
    bsz, t, _ = xx.shape
    nt = t // ROW_TILE
    row = pl.BlockSpec((None, ROW_TILE, D_MODEL), lambda b, i: (b, i, 0))
    mod = pl.BlockSpec((None, 2, D_MODEL), lambda b, i: (b, 0, 0))
    return pl.pallas_call(
        _ffn_kernel,
        grid=(bsz, nt),
        in_specs=[row, _const_spec((1, D_MODEL)), mod, mod, _const_spec((1, D_MODEL)), mod,
                  _const_spec(wg.shape), _const_spec(wu.shape), _const_spec(wo.shape)],
        out_specs=row,
        out_shape=jax.ShapeDtypeStruct(xx.shape, F32),
        compiler_params=_cparams(2),
        name="ffn",
    )(xx, gin, sh, sc, gout, gate, wg, wu, wo)


def _rope_tables(seq):
    rows = seq // GRID_W
    row = jnp.repeat(jnp.arange(rows, dtype=F32), GRID_W)
    col = jnp.tile(jnp.arange(GRID_W, dtype=F32), rows)
    nf = DA_HEAD_DIM // 4
    inv = ROPE_BASE ** (-jnp.arange(nf, dtype=F32) / nf)
    ang = jnp.concatenate([row[:, None] * inv, col[:, None] * inv], axis=-1)
    cos = jnp.concatenate([jnp.ones((CTX, DA_HEAD_DIM // 2), F32), jnp.cos(ang)], axis=0)
    sin = jnp.concatenate([jnp.zeros((CTX, DA_HEAD_DIM // 2), F32), jnp.sin(ang)], axis=0)
    return jnp.tile(cos, (1, 4)), jnp.tile(jnp.concatenate([-sin, sin], axis=1), (1, 2))


def kernel(x, c, ctx, c_ctx, w_ada, b_ada, norm_g, w_in, conv_w, conv_b, a_log, dt_bias,
           d_skip, ssd_norm_g, diff_lambda, subln_g, w_out, w_ffn_in, w_ffn_out):
    bsz, seq, _ = x.shape
    depth = w_ada.shape[0]
    t = CTX + seq
    assert ctx.shape[1] == CTX and t % ROW_TILE == 0 and seq % ATT_TQ == 0

    xx = jnp.concatenate([ctx, x], axis=1)
    cond = jnp.zeros((SUBLANES, D_MODEL), F32).at[:bsz].set(c).at[bsz].set(c_ctx)
    mod = _modulation(cond, w_ada, b_ada)
    cos_t, sin_t = _rope_tables(seq)

    q_end = DA_WIDTH
    k_end = 2 * DA_WIDTH
    v_end = 3 * DA_WIDTH
    z_end = v_end + SSD_WIDTH
    xbc_end = z_end + SSD_CONV_CH

    for i in range(depth):
        lam_init = 0.8 - 0.6 * math.exp(-0.3 * i)
        mods = [jnp.stack([mod[i, :bsz, j * D_MODEL:(j + 1) * D_MODEL],
                           jnp.broadcast_to(mod[i, bsz, j * D_MODEL:(j + 1) * D_MODEL],
                                            (bsz, D_MODEL))], axis=1) for j in range(N_MOD)]
        sh1, sc1, g1, sh2, sc2, g2 = mods
        wi = w_in[i].astype(BF16)
        wdt = jnp.zeros((D_MODEL, LANES), BF16).at[:, :2 * SSD_HEADS].set(wi[:, xbc_end:])
        lane16 = lambda a: jnp.zeros((1, LANES), F32).at[0, :2 * SSD_HEADS].set(a.reshape(-1))
        q, k, v, z, xbc, dt, dtt = _inproj(
            xx, norm_g[i, 0][None], sh1, sc1, cos_t, sin_t, lane16(dt_bias[i]),
            wi[:, :q_end], wi[:, q_end:k_end], wi[:, k_end:v_end], wi[:, v_end:z_end],
            wi[:, z_end:xbc_end], wdt)
        cw = jnp.zeros((SUBLANES, SSD_CONV_CH), F32).at[:SSD_CONV].set(conv_w[i])
        xc = _conv(xbc, cw, conv_b[i][None])
        yf, yb = _ssd(xc, dt, dtt, lane16(a_log[i]), a_log[i].reshape(2 * SSD_HEADS, 1))
        attn_ctx, attn_lat = _attention(q, k, v, diff_lambda[i], subln_g[i][None], lam_init)
        wo = w_out[i].astype(BF16)
        xx = _mixout(xx, attn_ctx, attn_lat, yf, yb, xc, z,
                     jnp.repeat(d_skip[i], SSD_HEAD_DIM)[None], ssd_norm_g[i][None],
                     norm_g[i, 1][None], g1, wo[:DA_WIDTH], wo[DA_WIDTH:])
        wf = w_ffn_in[i].astype(BF16)
        xx = _ffn(xx, norm_g[i, 2][None], sh2, sc2, norm_g[i, 3][None], g2,
                  wf[:, :D_FF], wf[:, D_FF:], w_ffn_out[i].astype(BF16))
    return xx[:, CTX:]
```
